```python
import jax, jax.numpy as jnp
from jax import lax
import numpy as np

D_MODEL = 2048
BATCH = 4
SEQ = 4096
DEPTH = 2

GRID_W = 64
CTX_LEN = 256
HEAD_DIM = 128
ROPE_THETA = 10000.0
BLOCK = 128
A_HEADS = 8
A_KV_HEADS = 2
A_GROUP = A_HEADS // A_KV_HEADS
B_HEADS = 8
B_Q_RANK = 512
B_KV_RANK = 256
B_NOPE = 128
B_ROPE = 64
B_VDIM = 128
B_QK = B_NOPE + B_ROPE
C_CH = 1024
CONV_K = 31
D_HEADS = 8
D_KV_HEADS = 2
D_GROUP = D_HEADS // D_KV_HEADS
WINDOW = 128
N_BRANCH = 4
A_W = A_HEADS * HEAD_DIM
A_KV_W = A_KV_HEADS * HEAD_DIM
B_W = B_HEADS * B_VDIM
D_W = D_HEADS * HEAD_DIM
D_KV_W = D_KV_HEADS * HEAD_DIM
OFF_AK = 0
OFF_AV = OFF_AK + A_KV_W
OFF_BCKV = OFF_AV + A_KV_W
OFF_BKR = OFF_BCKV + B_KV_RANK
OFF_DK = OFF_BKR + B_ROPE
OFF_DV = OFF_DK + D_KV_W
KV_COLS = OFF_DV + D_KV_W
OFF_AQ = KV_COLS
OFF_BCQ = OFF_AQ + A_W
OFF_DQ = OFF_BCQ + B_Q_RANK
OFF_GLU = OFF_DQ + D_W
OFF_GATE = OFF_GLU + 2 * C_CH
IN_COLS = OFF_GATE + N_BRANCH * D_MODEL
PEER_HEADS = 8
PEER_NKEYS = 128
PEER_EXPERTS = PEER_NKEYS * PEER_NKEYS
PEER_TOPK = 16
PEER_QDIM = 256
PEER_CHUNK = 128
ALPHA = (2 * DEPTH) ** 0.25
BETA = (8 * DEPTH) ** -0.25
SCALE_HD = HEAD_DIM ** -0.5
SCALE_MLA = B_QK ** -0.5
RMS_EPS = 1e-6
LN_EPS = 1e-5
NEG_INF = -1e30

kernel_name = 'hybrid_gated_branch_diffusion_block'


def _cols(p, off, n):
    return p[..., off:off + n]


def rms_norm(x, g):
    x32 = x.astype(jnp.float32)
    y = x32 * lax.rsqrt(jnp.mean(x32 * x32, axis=-1, keepdims=True) + RMS_EPS)
    return (y * g.astype(jnp.float32)).astype(x.dtype)


def layer_norm(x, g, b):
    x32 = x.astype(jnp.float32)
    mu = jnp.mean(x32, axis=-1, keepdims=True)
    xc = x32 - mu
    y = xc * lax.rsqrt(jnp.mean(xc * xc, axis=-1, keepdims=True) + LN_EPS)
    return (y * g.astype(jnp.float32) + b.astype(jnp.float32)).astype(x.dtype)


def axial_rope_tables(row, col, dim):
    half = dim // 2
    freq = ROPE_THETA ** (-jnp.arange(0, half, 2, dtype=jnp.float32) / half)
    ang = jnp.concatenate([row[:, None] * freq, col[:, None] * freq], axis=-1)
    return jnp.cos(ang), jnp.sin(ang)


def apply_rope(x, cos, sin):
    half = x.shape[-1] // 2
    shape = (1, x.shape[1]) + (1,) * (x.ndim - 3) + (half,)
    cos = cos.reshape(shape).astype(x.dtype)
    sin = sin.reshape(shape).astype(x.dtype)
    x1, x2 = x[..., :half], x[..., half:]
    return jnp.concatenate([x1 * cos - x2 * sin, x2 * cos + x1 * sin], axis=-1)


def kv_parts(p, a_k_norm, b_kv_norm, b_w_kv_up, rope):
    bsz, L = p.shape[:2]
    ka = rms_norm(_cols(p, OFF_AK, A_KV_W).reshape(bsz, L, A_KV_HEADS, HEAD_DIM), a_k_norm)
    va = _cols(p, OFF_AV, A_KV_W).reshape(bsz, L, A_KV_HEADS, HEAD_DIM)
    ckv = rms_norm(_cols(p, OFF_BCKV, B_KV_RANK), b_kv_norm)
    kv_up = (ckv @ b_w_kv_up).reshape(bsz, L, B_HEADS, B_NOPE + B_VDIM)
    kr = _cols(p, OFF_BKR, B_ROPE)
    kd = _cols(p, OFF_DK, D_KV_W).reshape(bsz, L, D_KV_HEADS, HEAD_DIM)
    vd = _cols(p, OFF_DV, D_KV_W).reshape(bsz, L, D_KV_HEADS, HEAD_DIM)
    if rope is not None:
        cos_h, sin_h, cos_r, sin_r = rope
        ka = apply_rope(ka, cos_h, sin_h)
        kr = apply_rope(kr, cos_r, sin_r)
        kd = apply_rope(kd, cos_h, sin_h)
    kb = jnp.concatenate([kv_up[..., :B_NOPE], jnp.broadcast_to(kr[:, :, None, :], (bsz, L, B_HEADS, B_ROPE))], axis=-1)
    vb = kv_up[..., B_NOPE:]
    return ka, va, kb, vb, kd, vd


def q_parts(p, a_q_norm, b_q_norm, b_w_q_up, rope):
    bsz, L = p.shape[:2]
    qa = rms_norm(_cols(p, OFF_AQ, A_W).reshape(bsz, L, A_KV_HEADS, A_GROUP, HEAD_DIM), a_q_norm)
    cq = rms_norm(_cols(p, OFF_BCQ, B_Q_RANK), b_q_norm)
    qb = (cq @ b_w_q_up).reshape(bsz, L, B_HEADS, B_QK)
    qd = _cols(p, OFF_DQ, D_W).reshape(bsz, L, D_KV_HEADS, D_GROUP, HEAD_DIM)
    if rope is not None:
        cos_h, sin_h, cos_r, sin_r = rope
        qa = apply_rope(qa, cos_h, sin_h)
        qb = jnp.concatenate([qb[..., :B_NOPE], apply_rope(qb[..., B_NOPE:], cos_r, sin_r)], axis=-1)
        qd = apply_rope(qd, cos_h, sin_h)
    return qa, qb[:, :, :, None, :], qd


def attend(q, k, v, scale, sink=None, mask=None):
    s = jnp.einsum('bqhgd,bkhd->bhgqk', q, k).astype(jnp.float32) * scale
    if mask is not None:
        s = jnp.where(mask, s, NEG_INF)
    if sink is not None:
        kvh, grp = q.shape[2], q.shape[3]
        sink_col = jnp.broadcast_to(sink.astype(jnp.float32).reshape(1, kvh, grp, 1, 1), s.shape[:-1] + (1,))
        s = jnp.concatenate([s, sink_col], axis=-1)
    pr = jax.nn.softmax(s, axis=-1)
    if sink is not None:
        pr = pr[..., :-1]
    return jnp.einsum('bhgqk,bkhd->bqhgd', pr.astype(v.dtype), v)


def dense_attn(q, k, v, scale):
    bsz, L = q.shape[:2]
    nb = L // BLOCK
    qb = jnp.moveaxis(q.reshape((bsz, nb, BLOCK) + q.shape[2:]), 1, 0)
    o = lax.map(lambda qi: attend(qi, k, v, scale), qb)
    return jnp.moveaxis(o, 0, 1).reshape(bsz, L, -1)


def window_attn(q, k, v, kc, vc, sink, scale):
    bsz, L = q.shape[:2]
    nb = L // BLOCK
    lc = kc.shape[1]
    kp = jnp.pad(k, ((0, 0), (BLOCK, BLOCK), (0, 0), (0, 0)))
    vp = jnp.pad(v, ((0, 0), (BLOCK, BLOCK), (0, 0), (0, 0)))
    qb = jnp.moveaxis(q.reshape((bsz, nb, BLOCK) + q.shape[2:]), 1, 0)
    ctx_ok = jnp.ones((BLOCK, lc), dtype=bool)

    def one_block(args):
        qi, n = args
        start = n * BLOCK
        kw = lax.dynamic_slice_in_dim(kp, start, 3 * BLOCK, axis=1)
        vw = lax.dynamic_slice_in_dim(vp, start, 3 * BLOCK, axis=1)
        qpos = start + jnp.arange(BLOCK)
        kpos = start - BLOCK + jnp.arange(3 * BLOCK)
        band = (jnp.abs(qpos[:, None] - kpos[None, :]) <= WINDOW) & (kpos >= 0)[None, :] & (kpos < L)[None, :]
        mask = jnp.concatenate([ctx_ok, band], axis=1)
        return attend(qi, jnp.concatenate([kc, kw], axis=1), jnp.concatenate([vc, vw], axis=1), scale, sink, mask)

    o = lax.map(one_block, (qb, jnp.arange(nb)))
    return jnp.moveaxis(o, 0, 1).reshape(bsz, L, -1)


def conformer_conv(pglu, conv_w, conv_b, ln_g, ln_b):
    a, gt = jnp.split(pglu, 2, axis=-1)
    u = a * jax.nn.sigmoid(gt)
    y = lax.conv_general_dilated(u, conv_w.reshape(CONV_K, 1, C_CH), window_strides=(1,),
                                 padding=[(CONV_K // 2, CONV_K // 2)],
                                 dimension_numbers=('NWC', 'WIO', 'NWC'),
                                 feature_group_count=C_CH) + conv_b
    return jax.nn.silu(layer_norm(y, ln_g, ln_b))


def merge_branches(branches, pgate, w_brs, w_o):
    gates = pgate.reshape(pgate.shape[:-1] + (N_BRANCH, D_MODEL))
    acc = None
    for n in range(N_BRANCH):
        term = jax.nn.sigmoid(gates[..., n, :]) * (branches[n] @ w_brs[n])
        acc = term if acc is None else acc + term
    return acc @ w_o


def peer_ffn(h, wq, subkeys, u, v):
    bsz, L, dm = h.shape
    half = PEER_QDIM // 2
    n_cand = PEER_TOPK * PEER_TOPK

    def one_chunk(xc):
        q = (xc @ wq).reshape(PEER_CHUNK, PEER_HEADS, 2, half)
        s = jnp.einsum('chpd,hpnd->chpn', q, subkeys).astype(jnp.float32)
        s_top, i_top = lax.top_k(s, PEER_TOPK)
        cand_s = (s_top[:, :, 0, :, None] + s_top[:, :, 1, None, :]).reshape(PEER_CHUNK, PEER_HEADS, n_cand)
        cand_i = (i_top[:, :, 0, :, None] * PEER_NKEYS + i_top[:, :, 1, None, :]).reshape(PEER_CHUNK, PEER_HEADS, n_cand)
        best_s, best_pos = lax.top_k(cand_s, PEER_TOPK)
        idx = jnp.take_along_axis(cand_i, best_pos, axis=-1)
        g = jax.nn.softmax(best_s, axis=-1).astype(xc.dtype)
        act = jax.nn.gelu(jnp.einsum('cd,chkd->chk', xc, u[idx]), approximate=False)
        return jnp.einsum('chk,chkd->cd', g * act, v[idx])

    out = lax.map(one_chunk, h.reshape((bsz * L) // PEER_CHUNK, PEER_CHUNK, dm))
    return out.reshape(bsz, L, dm)


def setup_inputs(seed: int = 0) -> dict:
    key = jax.random.key(seed)
    ks = iter(jax.random.split(key, 40))
    f32 = jnp.float32

    def nrm(shape, scale):
        return jax.random.normal(next(ks), shape, f32) * scale

    def gain(shape):
        return 1.0 + nrm(shape, 0.02)

    L = DEPTH
    return {
        'x': nrm((BATCH, SEQ, D_MODEL), 1.0),
        'c': nrm((BATCH, D_MODEL), 1.0),
        'ctx': nrm((BATCH, CTX_LEN, D_MODEL), 1.0),
        'c_ctx': nrm((D_MODEL,), 1.0),
        'w_ada': nrm((L, D_MODEL, 6 * D_MODEL), 0.5 * D_MODEL ** -0.5),
        'b_ada': nrm((L, 6 * D_MODEL), 0.02),
        'w_in': nrm((L, D_MODEL, IN_COLS), D_MODEL ** -0.5),
        'a_q_norm': gain((L, HEAD_DIM)),
        'a_k_norm': gain((L, HEAD_DIM)),
        'b_q_norm': gain((L, B_Q_RANK)),
        'b_w_q_up': nrm((L, B_Q_RANK, B_HEADS * B_QK), B_Q_RANK ** -0.5),
        'b_kv_norm': gain((L, B_KV_RANK)),
        'b_w_kv_up': nrm((L, B_KV_RANK, B_HEADS * (B_NOPE + B_VDIM)), B_KV_RANK ** -0.5),
        'c_conv_w': nrm((L, CONV_K, C_CH), CONV_K ** -0.5),
        'c_conv_b': nrm((L, C_CH), 0.02),
        'c_ln_g': gain((L, C_CH)),
        'c_ln_b': nrm((L, C_CH), 0.02),
        'd_sink': nrm((L, D_HEADS), 0.5),
        'w_br_a': nrm((L, A_W, D_MODEL), A_W ** -0.5),
        'w_br_b': nrm((L, B_W, D_MODEL), B_W ** -0.5),
        'w_br_c': nrm((L, C_CH, D_MODEL), C_CH ** -0.5),
        'w_br_d': nrm((L, D_W, D_MODEL), D_W ** -0.5),
        'w_o': nrm((L, D_MODEL, D_MODEL), BETA * D_MODEL ** -0.5),
        'ln1_g': gain((L, D_MODEL)),
        'ln1_b': nrm((L, D_MODEL), 0.02),
        'ln2_g': gain((L, D_MODEL)),
        'ln2_b': nrm((L, D_MODEL), 0.02),
        'peer_wq': nrm((L, D_MODEL, PEER_HEADS * PEER_QDIM), D_MODEL ** -0.5),
        'peer_subkeys': nrm((L, PEER_HEADS, 2, PEER_NKEYS, PEER_QDIM // 2), (PEER_QDIM // 2) ** -0.5),
        'peer_u': nrm((L, PEER_EXPERTS, D_MODEL), D_MODEL ** -0.5),
        'peer_v': nrm((L, PEER_EXPERTS, D_MODEL), BETA),
    }


def reference(x, c, ctx, c_ctx, w_ada, b_ada, w_in, a_q_norm, a_k_norm, b_q_norm, b_w_q_up, b_kv_norm, b_w_kv_up,
              c_conv_w, c_conv_b, c_ln_g, c_ln_b, d_sink, w_br_a, w_br_b, w_br_c, w_br_d, w_o,
              ln1_g, ln1_b, ln2_g, ln2_b, peer_wq, peer_subkeys, peer_u, peer_v):
    bsz, S, _ = x.shape
    ROWS = S // GRID_W
    row = jnp.repeat(jnp.arange(ROWS, dtype=jnp.float32), GRID_W)
    col = jnp.tile(jnp.arange(GRID_W, dtype=jnp.float32), ROWS)
    cos_h, sin_h = axial_rope_tables(row, col, HEAD_DIM)
    cos_r, sin_r = axial_rope_tables(row, col, B_ROPE)
    rope_lat = (cos_h, sin_h, cos_r, sin_r)
    sc = jax.nn.silu(c)
    scc = jax.nn.silu(c_ctx)
    xc = ctx
    for l in range(DEPTH):
        last = l == DEPTH - 1
        sh1, s1, g1, sh2, s2, g2 = jnp.split((sc @ w_ada[l] + b_ada[l])[:, None, :], 6, axis=-1)
        csh1, cs1, cg1, csh2, cs2, cg2 = jnp.split(scc @ w_ada[l] + b_ada[l], 6, axis=-1)
        wl = w_in[l]
        w_brs = (w_br_a[l], w_br_b[l], w_br_c[l], w_br_d[l])
        hc = xc * (1 + cs1) + csh1
        pc = hc @ (wl[:, :KV_COLS] if last else wl)
        akc, avc, bkc, bvc, dkc, dvc = kv_parts(pc, a_k_norm[l], b_kv_norm[l], b_w_kv_up[l], None)
        h = x * (1 + s1) + sh1
        p = h @ wl
        ak, av, bk, bv, dk, dv = kv_parts(p, a_k_norm[l], b_kv_norm[l], b_w_kv_up[l], rope_lat)
        qa, qb, qd = q_parts(p, a_q_norm[l], b_q_norm[l], b_w_q_up[l], rope_lat)
        branches = (
            dense_attn(qa, jnp.concatenate([akc, ak], axis=1), jnp.concatenate([avc, av], axis=1), SCALE_HD),
            dense_attn(qb, jnp.concatenate([bkc, bk], axis=1), jnp.concatenate([bvc, bv], axis=1), SCALE_MLA),
            conformer_conv(_cols(p, OFF_GLU, 2 * C_CH), c_conv_w[l], c_conv_b[l], c_ln_g[l], c_ln_b[l]),
            window_attn(qd, dk, dv, dkc, dvc, d_sink[l], SCALE_HD),
        )
        mix = merge_branches(branches, _cols(p, OFF_GATE, N_BRANCH * D_MODEL), w_brs, w_o[l])
        x_new = layer_norm(ALPHA * x + g1 * mix, ln1_g[l], ln1_b[l])
        y = peer_ffn(x_new * (1 + s2) + sh2, peer_wq[l], peer_subkeys[l], peer_u[l], peer_v[l])
        x_new = layer_norm(ALPHA * x_new + g2 * y, ln2_g[l], ln2_b[l])
        if not last:
            lc = xc.shape[1]
            qac, qbc, qdc = q_parts(pc, a_q_norm[l], b_q_norm[l], b_w_q_up[l], None)
            branches_c = (
                attend(qac, akc, avc, SCALE_HD).reshape(bsz, lc, -1),
                attend(qbc, bkc, bvc, SCALE_MLA).reshape(bsz, lc, -1),
                conformer_conv(_cols(pc, OFF_GLU, 2 * C_CH), c_conv_w[l], c_conv_b[l], c_ln_g[l], c_ln_b[l]),
                attend(qdc, dkc, dvc, SCALE_HD, sink=d_sink[l]).reshape(bsz, lc, -1),
            )
            mixc = merge_branches(branches_c, _cols(pc, OFF_GATE, N_BRANCH * D_MODEL), w_brs, w_o[l])
            xc = layer_norm(ALPHA * xc + cg1 * mixc, ln1_g[l], ln1_b[l])
            yc = peer_ffn(xc * (1 + cs2) + csh2, peer_wq[l], peer_subkeys[l], peer_u[l], peer_v[l])
            xc = layer_norm(ALPHA * xc + cg2 * yc, ln2_g[l], ln2_b[l])
        x = x_new
    return x
```

```python
import functools

import jax
import jax.numpy as jnp
from jax import lax
from jax.experimental import pallas as pl
from jax.experimental.pallas import tpu as pltpu

F32 = jnp.float32
BF16 = jnp.bfloat16

GRID_W = 64
HEAD_DIM = 128
ROPE_THETA = 10000.0
A_HEADS, A_KV_HEADS = 8, 2
B_HEADS, B_Q_RANK, B_KV_RANK, B_NOPE, B_ROPE, B_VDIM = 8, 512, 256, 128, 64, 128
B_QK = B_NOPE + B_ROPE
C_CH, CONV_K = 1024, 31
D_HEADS, D_KV_HEADS = 8, 2
WINDOW = 128
N_BRANCH = 4
PEER_HEADS, PEER_NKEYS, PEER_TOPK, PEER_QDIM = 8, 128, 16, 256
RMS_EPS = 1e-6
LN_EPS = 1e-5
NEG_INF = -1e30

LANES = 128
TOK = 256
MM_TM = 512
MM_TN = 512
PEER_TT = 128
CONV_HALO = 16
VMEM_LIMIT = 56 * 1024 * 1024

P_AQ, P_DQ, P_GLU_A, P_GLU_G, P_GATE = 0, 1024, 2048, 3072, 4096
P_KV = 12288
KV_W = 1536
P_BCQ = 13824
NP_COLS = 14336


def _cparams(*sem):
    return pltpu.CompilerParams(dimension_semantics=sem, vmem_limit_bytes=VMEM_LIMIT)


def _layer_norm_rows(z, g, b):
    mu = jnp.mean(z, axis=-1, keepdims=True)
    zc = z - mu
    var = jnp.mean(zc * zc, axis=-1, keepdims=True)
    return zc * lax.rsqrt(var + LN_EPS) * g + b


def _mm_body(*refs, prologue, has_bias):
    refs = list(refs)
    x_ref = refs.pop(0)
    s_ref = sh_ref = b_ref = None
    if prologue == "mod":
        s_ref, sh_ref = refs.pop(0), refs.pop(0)
    w_ref = refs.pop(0)
    if has_bias:
        b_ref = refs.pop(0)
    o_ref, h_scr = refs

    @pl.when(pl.program_id(1) == 0)
    def _():
        xv = x_ref[...].astype(F32)
        if prologue == "mod":
            xv = xv * (1.0 + s_ref[0]) + sh_ref[0]
        elif prologue == "silu":
            xv = xv * jax.nn.sigmoid(xv)
        h_scr[...] = xv.astype(BF16)

    acc = jnp.dot(h_scr[...], w_ref[...].astype(BF16), preferred_element_type=F32)
    if has_bias:
        acc = acc + b_ref[...]
    o_ref[...] = acc.astype(o_ref.dtype)


def _mm(x, w, *, n_rows, tm, tn, out_dtype, prologue="none", mod=None, mod_row=None, bias=None, name=None):
    k_dim = x.shape[1]
    n_dim = w.shape[1]
    in_specs = [pl.BlockSpec((tm, k_dim), lambda i, j: (i, 0))]
    args = [x]
    if prologue == "mod":
        spec = pl.BlockSpec((1, 1, k_dim), lambda i, j: (mod_row(i), 0, 0))
        in_specs += [spec, spec]
        args += list(mod)
    in_specs.append(pl.BlockSpec((k_dim, tn), lambda i, j: (0, j)))
    args.append(w)
    if bias is not None:
        in_specs.append(pl.BlockSpec((1, tn), lambda i, j: (0, j)))
        args.append(bias)
    return pl.pallas_call(
        functools.partial(_mm_body, prologue=prologue, has_bias=bias is not None),
        grid=(n_rows // tm, n_dim // tn),
        in_specs=in_specs,
        out_specs=pl.BlockSpec((tm, tn), lambda i, j: (i, j)),
        out_shape=jax.ShapeDtypeStruct((n_rows, n_dim), out_dtype),
        scratch_shapes=[pltpu.VMEM((tm, k_dim), BF16)],
        compiler_params=_cparams("parallel", "arbitrary"),
        name=name,
    )(*args)


def _prep_body(aq_ref, dq_ref, kv_ref, cq_ref, cosh_ref, sinh_ref, cosr_ref, sinr_ref,
               aqn_ref, akn_ref, bqn_ref, bkvn_ref, wq_ref, wkv_ref,
               qa_ref, qb_ref, qd_ref, ka_ref, va_ref, kb_ref, vb_ref, kd_ref, vd_ref):
    cosh, sinh = cosh_ref[...], sinh_ref[...]
    cosr, sinr = cosr_ref[...], sinr_ref[...]
    lane = lax.broadcasted_iota(jnp.int32, (TOK, LANES), 1)
    low32 = (lane % 64) < 32
    low64 = lane < 64

    def rope_head(xh):
        return xh * cosh + pltpu.roll(xh, 64, 1) * sinh

    def rope_r64(xc):
        sw = jnp.where(low32, pltpu.roll(xc, 96, 1), pltpu.roll(xc, 32, 1))
        return xc * cosr + sw * sinr

    def rms(xh, g):
        ms = jnp.mean(xh * xh, axis=-1, keepdims=True)
        return xh * lax.rsqrt(ms + RMS_EPS) * g

    aqn, akn = aqn_ref[...], akn_ref[...]
    for h in range(A_HEADS):
        sl = slice(h * HEAD_DIM, (h + 1) * HEAD_DIM)
        qa_ref[:, sl] = rope_head(rms(aq_ref[:, sl], aqn)).astype(BF16)
        qd_ref[:, sl] = rope_head(dq_ref[:, sl]).astype(BF16)
    for h in range(A_KV_HEADS):
        sl = slice(h * HEAD_DIM, (h + 1) * HEAD_DIM)
        ka_ref[h] = rope_head(rms(kv_ref[:, sl], akn)).astype(BF16)
        va_ref[h] = kv_ref[:, 256 + h * HEAD_DIM:256 + (h + 1) * HEAD_DIM].astype(BF16)
        kd_ref[h] = rope_head(kv_ref[:, 768 + h * HEAD_DIM:768 + (h + 1) * HEAD_DIM]).astype(BF16)
        vd_ref[h] = kv_ref[:, 1024 + h * HEAD_DIM:1024 + (h + 1) * HEAD_DIM].astype(BF16)

    ckv = rms(kv_ref[:, 512:768], bkvn_ref[...]).astype(BF16)
    kv_up = jnp.dot(ckv, wkv_ref[...], preferred_element_type=F32)
    kr = rope_r64(kv_ref[:, 1280:1408]).astype(BF16)
    for h in range(B_HEADS):
        kb_ref[h, :, 0:B_NOPE] = kv_up[:, h * B_NOPE:(h + 1) * B_NOPE].astype(BF16)
        kb_ref[h, :, B_NOPE:2 * B_NOPE] = kr
        vb_ref[h] = kv_up[:, 1024 + h * B_VDIM:1024 + (h + 1) * B_VDIM].astype(BF16)

    cq = rms(cq_ref[...], bqn_ref[...]).astype(BF16)
    q_up = jnp.dot(cq, wq_ref[...], preferred_element_type=F32)
    for c in range(B_HEADS // 2):
        rc = rope_r64(q_up[:, 1024 + c * LANES:1024 + (c + 1) * LANES])
        for half in range(2):
            h = 2 * c + half
            piece = rc if half == 0 else pltpu.roll(rc, 64, 1)
            qb_ref[:, h * 256:h * 256 + B_NOPE] = q_up[:, h * B_NOPE:(h + 1) * B_NOPE].astype(BF16)
            qb_ref[:, h * 256 + B_NOPE:(h + 1) * 256] = jnp.where(low64, piece, 0.0).astype(BF16)


def _prep(p, tabs, norms, wq, wkv, *, nb, seq, n_lat_rows, n_rows):
    nql = seq // TOK
    n_lat_tiles = n_lat_rows // TOK
    t_len = seq + TOK

    def colspec(width, off):
        return pl.BlockSpec((TOK, width), lambda t: (t, off // width))

    def tabspec():
        return pl.BlockSpec((TOK, LANES), lambda t: (jnp.where(t < n_lat_tiles, t % nql, nql), 0))

    def full(a):
        return pl.BlockSpec(a.shape, lambda t: (0,) * a.ndim)

    def kvspec(heads, d):
        return pl.BlockSpec(
            (None, heads, TOK, d),
            lambda t: (jnp.where(t < n_lat_tiles, t // nql, t - n_lat_tiles), 0,
                       jnp.where(t < n_lat_tiles, t % nql, nql), 0))

    def rowspec(width):
        return pl.BlockSpec((TOK, width), lambda t: (t, 0))

    def kvshape(heads, d):
        return jax.ShapeDtypeStruct((nb, heads, t_len, d), BF16)

    return pl.pallas_call(
        _prep_body,
        grid=(n_rows // TOK,),
        in_specs=[colspec(1024, P_AQ), colspec(1024, P_DQ), colspec(KV_W, P_KV), colspec(512, P_BCQ),
                  tabspec(), tabspec(), tabspec(), tabspec(),
                  full(norms[0]), full(norms[1]), full(norms[2]), full(norms[3]), full(wq), full(wkv)],
        out_specs=[rowspec(1024), rowspec(2048), rowspec(1024),
                   kvspec(2, 128), kvspec(2, 128), kvspec(8, 256), kvspec(8, 128), kvspec(2, 128), kvspec(2, 128)],
        out_shape=[jax.ShapeDtypeStruct((n_rows, 1024), BF16), jax.ShapeDtypeStruct((n_rows, 2048), BF16),
                   jax.ShapeDtypeStruct((n_rows, 1024), BF16),
                   kvshape(2, 128), kvshape(2, 128), kvshape(8, 256), kvshape(8, 128), kvshape(2, 128), kvshape(2, 128)],
        compiler_params=_cparams("parallel"),
        name="qkv_prep",
    )(p, p, p, p, *tabs, *norms, wq, wkv)


def _softmax_step(q, k, v, scale, mask, carry):
    m, l, acc = carry
    s = lax.dot_general(q, k, (((1,), (1,)), ((), ())), preferred_element_type=F32) * scale
    if mask is not None:
        s = jnp.where(mask, s, NEG_INF)
    m_new = jnp.maximum(m, jnp.max(s, axis=-1, keepdims=True))
    alpha = jnp.exp(m - m_new)
    pr = jnp.exp(s - m_new)
    l = alpha * l + jnp.sum(pr, axis=-1, keepdims=True)
    acc = alpha * acc + jnp.dot(pr.astype(BF16), v, preferred_element_type=F32)
    return m_new, l, acc


def _attn_dense_body(q_ref, k_ref, v_ref, o_ref, *, group, dk, scale, nql):
    qi = pl.program_id(2)
    c0 = jnp.where(qi >= nql, nql, 0)
    for g in range(group):
        q = q_ref[:, g * dk:(g + 1) * dk]

        def step(c, carry, q=q):
            off = pl.multiple_of(c * TOK, TOK)
            return _softmax_step(q, k_ref[pl.ds(off, TOK), :], v_ref[pl.ds(off, TOK), :], scale, None, carry)

        init = (jnp.full((TOK, 1), NEG_INF, F32), jnp.zeros((TOK, 1), F32), jnp.zeros((TOK, HEAD_DIM), F32))
        _, l, acc = lax.fori_loop(c0, nql + 1, step, init)
        o_ref[:, g * HEAD_DIM:(g + 1) * HEAD_DIM] = (acc * (1.0 / l)).astype(o_ref.dtype)


def _attn_win_body(sink_ref, q_ref, k_ref, v_ref, o_ref, *, group, scale, nql):
    kvh = pl.program_id(1)
    qi = pl.program_id(2)
    is_lat = qi < nql
    qpos = qi * TOK + lax.broadcasted_iota(jnp.int32, (TOK, TOK), 0)
    col = lax.broadcasted_iota(jnp.int32, (TOK, TOK), 1)
    for g in range(group):
        q = q_ref[:, g * HEAD_DIM:(g + 1) * HEAD_DIM]
        sink = sink_ref[kvh * group + g]
        carry = (jnp.full((TOK, 1), sink, F32), jnp.ones((TOK, 1), F32), jnp.zeros((TOK, HEAD_DIM), F32))
        for j in range(3):
            c = qi - 1 + j
            valid = jnp.logical_and(jnp.logical_and(c >= 0, c < nql), is_lat)
            off = pl.multiple_of(jnp.clip(c, 0, nql - 1) * TOK, TOK)
            kpos = c * TOK + col
            mask = jnp.logical_and(jnp.abs(qpos - kpos) <= WINDOW, valid)
            carry = _softmax_step(q, k_ref[pl.ds(off, TOK), :], v_ref[pl.ds(off, TOK), :], scale, mask, carry)
        carry = _softmax_step(q, k_ref[nql * TOK:(nql + 1) * TOK, :], v_ref[nql * TOK:(nql + 1) * TOK, :],
                              scale, None, carry)
        _, l, acc = carry
        o_ref[:, g * HEAD_DIM:(g + 1) * HEAD_DIM] = (acc * (1.0 / l)).astype(o_ref.dtype)


def _attention(q, k, v, *, nb, seq, n_q_tiles, group, dk, scale, sink=None, name=None):
    nql = seq // TOK
    kvh = k.shape[1]
    t_len = k.shape[2]

    def qrow(b, qi):
        return jnp.where(qi < nql, b * nql + qi, nb * nql + b)

    q_spec = pl.BlockSpec((TOK, group * dk), lambda b, h, qi: (qrow(b, qi), h))
    k_spec = pl.BlockSpec((None, None, t_len, dk), lambda b, h, qi: (b, h, 0, 0))
    v_spec = pl.BlockSpec((None, None, t_len, HEAD_DIM), lambda b, h, qi: (b, h, 0, 0))
    o_spec = pl.BlockSpec((TOK, group * HEAD_DIM), lambda b, h, qi: (qrow(b, qi), h))
    n_rows_out = nb * seq + (nb * TOK if n_q_tiles > nql else 0)
    if sink is None:
        body = functools.partial(_attn_dense_body, group=group, dk=dk, scale=scale, nql=nql)
        in_specs, args = [q_spec, k_spec, v_spec], (q, k, v)
    else:
        body = functools.partial(_attn_win_body, group=group, scale=scale, nql=nql)
        in_specs = [pl.BlockSpec(memory_space=pltpu.SMEM), q_spec, k_spec, v_spec]
        args = (sink, q, k, v)
    return pl.pallas_call(
        body,
        grid=(nb, kvh, n_q_tiles),
        in_specs=in_specs,
        out_specs=o_spec,
        out_shape=jax.ShapeDtypeStruct((n_rows_out, kvh * group * HEAD_DIM), BF16),
        compiler_params=_cparams("parallel", "parallel", "arbitrary"),
        name=name,
    )(*args)


def _conv_body(a_ref, g_ref, ap_ref, gp_ref, an_ref, gn_ref, w_ref, cb_ref, lg_ref, lb_ref, o_ref, u_scr, y_scr,
               *, nql, n_lat_tiles):
    t = pl.program_id(0)
    is_lat = t < n_lat_tiles
    i = t % nql
    has_prev = jnp.logical_and(is_lat, i > 0)
    has_next = jnp.logical_and(is_lat, i < nql - 1)
    up = ap_ref[...] * jax.nn.sigmoid(gp_ref[...])
    un = an_ref[...] * jax.nn.sigmoid(gn_ref[...])
    u_scr[0:CONV_HALO, :] = jnp.where(has_prev, up, 0.0)
    u_scr[CONV_HALO:CONV_HALO + TOK, :] = a_ref[...] * jax.nn.sigmoid(g_ref[...])
    u_scr[CONV_HALO + TOK:, :] = jnp.where(has_next, un, 0.0)
    base = CONV_HALO - CONV_K // 2
    for lc in range(C_CH // LANES):
        ls = slice(lc * LANES, (lc + 1) * LANES)
        acc = jnp.zeros((TOK, LANES), F32)
        for k in range(CONV_K):
            acc = acc + u_scr[base + k:base + k + TOK, ls] * w_ref[k:k + 1, ls]
        y_scr[:, ls] = acc + cb_ref[:, ls]
    y = _layer_norm_rows(y_scr[...], lg_ref[...], lb_ref[...])
    o_ref[...] = (y * jax.nn.sigmoid(y)).astype(o_ref.dtype)


def _conv(p, conv_w, conv_b, ln_g, ln_b, *, seq, n_lat_rows, n_rows):
    nql = seq // TOK
    hb = TOK // CONV_HALO
    last_hb = p.shape[0] // CONV_HALO - 1

    def cur(off):
        return pl.BlockSpec((TOK, C_CH), lambda t: (t, off // C_CH))

    def prev(off):
        return pl.BlockSpec((CONV_HALO, C_CH), lambda t: (jnp.maximum(t * hb - 1, 0), off // C_CH))

    def nxt(off):
        return pl.BlockSpec((CONV_HALO, C_CH), lambda t: (jnp.minimum((t + 1) * hb, last_hb), off // C_CH))

    def full(a):
        return pl.BlockSpec(a.shape, lambda t: (0,) * a.ndim)

    return pl.pallas_call(
        functools.partial(_conv_body, nql=nql, n_lat_tiles=n_lat_rows // TOK),
        grid=(n_rows // TOK,),
        in_specs=[cur(P_GLU_A), cur(P_GLU_G), prev(P_GLU_A), prev(P_GLU_G), nxt(P_GLU_A), nxt(P_GLU_G),
                  full(conv_w), full(conv_b), full(ln_g), full(ln_b)],
        out_specs=pl.BlockSpec((TOK, C_CH), lambda t: (t, 0)),
        out_shape=jax.ShapeDtypeStruct((n_rows, C_CH), BF16),
        scratch_shapes=[pltpu.VMEM((TOK + 2 * CONV_HALO, C_CH), F32), pltpu.VMEM((TOK, C_CH), F32)],
        compiler_params=_cparams("parallel"),
        name="conformer_conv",
    )(p, p, p, p, p, p, conv_w, conv_b, ln_g, ln_b)


def _merge_body(xa_ref, xb_ref, xc_ref, xd_ref, wa_ref, wb_ref, wc_ref, wd_ref,
                ga_ref, gb_ref, gc_ref, gd_ref, o_ref):
    acc = None
    for x_ref, w_ref, g_ref in ((xa_ref, wa_ref, ga_ref), (xb_ref, wb_ref, gb_ref),
                                (xc_ref, wc_ref, gc_ref), (xd_ref, wd_ref, gd_ref)):
        term = jax.nn.sigmoid(g_ref[...]) * jnp.dot(x_ref[...], w_ref[...], preferred_element_type=F32)
        acc = term if acc is None else acc + term
    o_ref[...] = acc.astype(o_ref.dtype)


def _merge(branches, w_brs, p, *, n_rows, d_model):
    tm, tn = MM_TM, MM_TN
    x_spec = pl.BlockSpec((tm, 1024), lambda i, j: (i, 0))
    w_spec = pl.BlockSpec((1024, tn), lambda i, j: (0, j))

    def gate_spec(n):
        return pl.BlockSpec((tm, tn), lambda i, j: (i, (P_GATE + n * d_model) // tn + j))

    return pl.pallas_call(
        _merge_body,
        grid=(n_rows // tm, d_model // tn),
        in_specs=[x_spec] * 4 + [w_spec] * 4 + [gate_spec(n) for n in range(N_BRANCH)],
        out_specs=pl.BlockSpec((tm, tn), lambda i, j: (i, j)),
        out_shape=jax.ShapeDtypeStruct((n_rows, d_model), BF16),
        compiler_params=_cparams("parallel", "arbitrary"),
        name="branch_merge",
    )(*branches, *w_brs, p, p, p, p)


def _wo_ln_body(acc_ref, w_ref, x_ref, g1_ref, lg_ref, lb_ref, o_ref, *, alpha):
    mix = jnp.dot(acc_ref[...], w_ref[...], preferred_element_type=F32)
    z = alpha * x_ref[...] + g1_ref[0] * mix
    o_ref[...] = _layer_norm_rows(z, lg_ref[...], lb_ref[...])


def _wo_ln(acc, w_o, x_all, g1, ln_g, ln_b, *, n_rows, mod_row, alpha):
    d_model = w_o.shape[0]
    row = pl.BlockSpec((TOK, d_model), lambda i: (i, 0))
    vec = pl.BlockSpec((1, d_model), lambda i: (0, 0))
    return pl.pallas_call(
        functools.partial(_wo_ln_body, alpha=alpha),
        grid=(n_rows // TOK,),
        in_specs=[row, pl.BlockSpec(w_o.shape, lambda i: (0, 0)), row,
                  pl.BlockSpec((1, 1, d_model), lambda i: (mod_row(i), 0, 0)), vec, vec],
        out_specs=row,
        out_shape=jax.ShapeDtypeStruct((n_rows, d_model), F32),
        compiler_params=_cparams("parallel"),
        name="wo_deepnorm",
    )(acc, w_o, x_all, g1, ln_g, ln_b)


def _topk_rows(s, n_out, rowi):
    n = s.shape[0]
    vals, poss = [], []
    for _ in range(n_out):
        m = jnp.max(s, axis=0, keepdims=True)
        pos = jnp.min(jnp.where(s == m, rowi, n), axis=0, keepdims=True)
        s = jnp.where(rowi == pos, -jnp.inf, s)
        vals.append(m)
        poss.append(pos)
    return vals, poss


def _peer_topk_body(q_ref, sk_ref, idx_ref, g_ref, ts_scr, ti_scr, cs_scr, ci_scr):
    k = PEER_TOPK
    row_keys = lax.broadcasted_iota(jnp.int32, (PEER_NKEYS, TOK), 0)
    row_cand = lax.broadcasted_iota(jnp.int32, (k * k, TOK), 0)
    for h in range(PEER_HEADS):
        for part in range(2):
            c = h * 2 + part
            s_t = lax.dot_general(sk_ref[h, part], q_ref[:, c * LANES:(c + 1) * LANES],
                                  (((1,), (1,)), ((), ())), preferred_element_type=F32)
            vals, poss = _topk_rows(s_t, k, row_keys)
            for a in range(k):
                ts_scr[part, a:a + 1, :] = vals[a]
                ti_scr[part, a:a + 1, :] = poss[a]
        s2 = ts_scr[1]
        i2 = ti_scr[1]
        for a in range(k):
            cs_scr[a * k:(a + 1) * k, :] = ts_scr[0, a:a + 1, :] + s2
            ci_scr[a * k:(a + 1) * k, :] = ti_scr[0, a:a + 1, :] * PEER_NKEYS + i2
        vals, poss = _topk_rows(cs_scr[...], k, row_cand)
        cand_i = ci_scr[...]
        exps = [jnp.exp(v - vals[0]) for v in vals]
        denom = exps[0]
        for e in exps[1:]:
            denom = denom + e
        inv = 1.0 / denom
        for a in range(k):
            eid = jnp.sum(jnp.where(row_cand == poss[a], cand_i, 0), axis=0, keepdims=True)
            idx_ref[h * k + a:h * k + a + 1, :] = eid
            g_ref[h * k + a:h * k + a + 1, :] = exps[a] * inv


def _peer_topk(q, subkeys, *, n_rows):
    k = PEER_TOPK
    n_sel = PEER_HEADS * k
    return pl.pallas_call(
        _peer_topk_body,
        grid=(n_rows // TOK,),
        in_specs=[pl.BlockSpec((TOK, q.shape[1]), lambda t: (t, 0)),
                  pl.BlockSpec(subkeys.shape, lambda t: (0, 0, 0, 0))],
        out_specs=[pl.BlockSpec((n_sel, TOK), lambda t: (0, t)), pl.BlockSpec((n_sel, TOK), lambda t: (0, t))],
        out_shape=[jax.ShapeDtypeStruct((n_sel, n_rows), jnp.int32), jax.ShapeDtypeStruct((n_sel, n_rows), F32)],
        scratch_shapes=[pltpu.VMEM((2, k, TOK), F32), pltpu.VMEM((2, k, TOK), jnp.int32),
                        pltpu.VMEM((k * k, TOK), F32), pltpu.VMEM((k * k, TOK), jnp.int32)],
        compiler_params=_cparams("parallel"),
        name="peer_topk",
    )(q, subkeys)


def _peer_gather_body(idx_ref, g_ref, x_ref, s2_ref, sh2_ref, g2_ref, lg_ref, lb_ref, u_hbm, v_hbm, o_ref,
                      ubuf, vbuf, h_scr, y_scr, sem, *, alpha):
    n_sel = ubuf.shape[1]
    h_scr[...] = x_ref[...] * (1.0 + s2_ref[0]) + sh2_ref[0]

    def row_copy(table, buf, which, e, slot, k):
        return pltpu.make_async_copy(table.at[pl.ds(e, 1)], buf.at[slot, pl.ds(k, 1)], sem.at[which, slot])

    def issue(t, slot):
        for k in range(n_sel):
            e = idx_ref[t, k]
            row_copy(u_hbm, ubuf, 0, e, slot, k).start()
            row_copy(v_hbm, vbuf, 1, e, slot, k).start()

    def wait(slot):
        pltpu.make_async_copy(u_hbm.at[pl.ds(0, n_sel)], ubuf.at[slot], sem.at[0, slot]).wait()
        pltpu.make_async_copy(v_hbm.at[pl.ds(0, n_sel)], vbuf.at[slot], sem.at[1, slot]).wait()

    issue(0, 0)
    lane = lax.broadcasted_iota(jnp.int32, (n_sel, PEER_TT), 1)

    def body(t, carry):
        slot = lax.rem(t, 2)

        @pl.when(t + 1 < PEER_TT)
        def _():
            issue(t + 1, 1 - slot)

        wait(slot)
        h = h_scr[pl.ds(t, 1), :]
        a = jnp.sum(ubuf[slot] * h, axis=1, keepdims=True)
        act = 0.5 * a * (1.0 + lax.erf(a * 0.7071067811865476))
        gcol = jnp.sum(jnp.where(lane == t, g_ref[...], 0.0), axis=1, keepdims=True)
        y_scr[pl.ds(t, 1), :] = jnp.sum(vbuf[slot] * (act * gcol), axis=0, keepdims=True)
        return carry

    lax.fori_loop(0, PEER_TT, body, 0)
    z = alpha * x_ref[...] + g2_ref[0] * y_scr[...]
    o_ref[...] = _layer_norm_rows(z, lg_ref[...], lb_ref[...])


def _peer_gather(idx_t, g_t, x1, mods, ln_g, ln_b, u, v, *, n_rows, mod_row, alpha):
    d_model = x1.shape[1]
    n_sel = idx_t.shape[1]
    row = pl.BlockSpec((PEER_TT, d_model), lambda t: (t, 0))
    vec = pl.BlockSpec((1, d_model), lambda t: (0, 0))
    mod = pl.BlockSpec((1, 1, d_model), lambda t: (mod_row(t), 0, 0))
    return pl.pallas_call(
        functools.partial(_peer_gather_body, alpha=alpha),
        grid=(n_rows // PEER_TT,),
        in_specs=[pl.BlockSpec((PEER_TT, n_sel), lambda t: (t, 0), memory_space=pltpu.SMEM),
                  pl.BlockSpec((n_sel, PEER_TT), lambda t: (0, t)),
                  row, mod, mod, mod, vec, vec,
                  pl.BlockSpec(memory_space=pl.ANY), pl.BlockSpec(memory_space=pl.ANY)],
        out_specs=row,
        out_shape=jax.ShapeDtypeStruct((n_rows, d_model), F32),
        scratch_shapes=[pltpu.VMEM((2, n_sel, d_model), F32), pltpu.VMEM((2, n_sel, d_model), F32),
                        pltpu.VMEM((PEER_TT, d_model), F32), pltpu.VMEM((PEER_TT, d_model), F32),
                        pltpu.SemaphoreType.DMA((2, 2))],
        compiler_params=_cparams("arbitrary"),
        name="peer_gather",
    )(idx_t, g_t, x1, *mods, ln_g, ln_b, u, v)


def _rope_tables(seq):
    rows = seq // GRID_W
    row = jnp.repeat(jnp.arange(rows, dtype=F32), GRID_W)
    col = jnp.tile(jnp.arange(GRID_W, dtype=F32), rows)

    def tables(dim):
        half = dim // 2
        freq = ROPE_THETA ** (-jnp.arange(0, half, 2, dtype=F32) / half)
        ang = jnp.concatenate([row[:, None] * freq, col[:, None] * freq], axis=-1)
        return jnp.cos(ang), jnp.sin(ang)

    def with_identity(cos_t, sin_t):
        return (jnp.concatenate([cos_t, jnp.ones((TOK, LANES), F32)], axis=0),
                jnp.concatenate([sin_t, jnp.zeros((TOK, LANES), F32)], axis=0))

    cos_h, sin_h = tables(HEAD_DIM)
    cos_r, sin_r = tables(B_ROPE)
    cosh, sinh = with_identity(jnp.concatenate([cos_h, cos_h], -1), jnp.concatenate([-sin_h, sin_h], -1))
    cosr, sinr = with_identity(jnp.tile(cos_r, (1, 4)), jnp.tile(jnp.concatenate([-sin_r, sin_r], -1), (1, 2)))
    return cosh, sinh, cosr, sinr


def _permute_w_in(w):
    d_model = w.shape[0]
    a_w, a_kv_w = A_HEADS * HEAD_DIM, A_KV_HEADS * HEAD_DIM
    d_w, d_kv_w = D_HEADS * HEAD_DIM, D_KV_HEADS * HEAD_DIM
    off_ak = 0
    off_av = off_ak + a_kv_w
    off_bckv = off_av + a_kv_w
    off_bkr = off_bckv + B_KV_RANK
    off_dk = off_bkr + B_ROPE
    off_dv = off_dk + d_kv_w
    off_aq = off_dv + d_kv_w
    off_bcq = off_aq + a_w
    off_dq = off_bcq + B_Q_RANK
    off_glu = off_dq + d_w
    off_gate = off_glu + 2 * C_CH

    def seg(off, n):
        return w[:, off:off + n]

    out = jnp.concatenate(
        [seg(off_aq, a_w), seg(off_dq, d_w), seg(off_glu, 2 * C_CH), seg(off_gate, N_BRANCH * d_model),
         seg(off_ak, a_kv_w), seg(off_av, a_kv_w), seg(off_bckv, B_KV_RANK), seg(off_dk, d_kv_w), seg(off_dv, d_kv_w),
         seg(off_bkr, B_ROPE), jnp.zeros((d_model, KV_W - 1280 - B_ROPE), w.dtype), seg(off_bcq, B_Q_RANK)], axis=1)
    assert out.shape[1] == NP_COLS
    return out.astype(BF16)


def kernel(x, c, ctx, c_ctx, w_ada, b_ada, w_in, a_q_norm, a_k_norm, b_q_norm, b_w_q_up, b_kv_norm, b_w_kv_up,
           c_conv_w, c_conv_b, c_ln_g, c_ln_b, d_sink, w_br_a, w_br_b, w_br_c, w_br_d, w_o,
           ln1_g, ln1_b, ln2_g, ln2_b, peer_wq, peer_subkeys, peer_u, peer_v):
    nb, seq, d_model = x.shape
    depth = w_ada.shape[0]
    assert ctx.shape[1] == TOK and seq % MM_TM == 0 and (nb * TOK) % MM_TM == 0
    assert d_model == 2048 and P_GATE + N_BRANCH * d_model == P_KV
    rl, rc = nb * seq, nb * TOK
    r_all = rl + rc
    nql = seq // TOK
    alpha = (2 * depth) ** 0.25
    scale_hd = HEAD_DIM ** -0.5
    scale_mla = B_QK ** -0.5

    def mod_row(tile):
        return lambda i: jnp.where(i < rl // tile, i // (seq // tile), nb)

    tabs = _rope_tables(seq)
    x_all = jnp.concatenate([x.reshape(rl, d_model), ctx.reshape(rc, d_model)], axis=0)
    n_mod = 16
    cvec = jnp.concatenate([c, c_ctx[None, :], jnp.zeros((n_mod - nb - 1, d_model), F32)], axis=0)

    for l in range(depth):
        last = l == depth - 1
        n_rows = rl if last else r_all
        n_q_tiles = nql if last else nql + 1

        mod = _mm(cvec, w_ada[l], n_rows=n_mod, tm=n_mod, tn=MM_TN, out_dtype=F32, prologue="silu",
                  bias=b_ada[l][None, :], name="adaln")
        sh1, s1, g1, sh2, s2, g2 = [mod[:, i * d_model:(i + 1) * d_model].reshape(n_mod, 1, d_model)
                                    for i in range(6)]

        p = _mm(x_all, _permute_w_in(w_in[l]), n_rows=r_all, tm=MM_TM, tn=MM_TN, out_dtype=F32,
                prologue="mod", mod=(s1, sh1), mod_row=mod_row(MM_TM), name="in_proj")

        wq = b_w_q_up[l].reshape(B_Q_RANK, B_HEADS, B_QK)
        wq = jnp.concatenate([wq[:, :, :B_NOPE].reshape(B_Q_RANK, -1), wq[:, :, B_NOPE:].reshape(B_Q_RANK, -1)],
                             axis=1).astype(BF16)
        wkv = b_w_kv_up[l].reshape(B_KV_RANK, B_HEADS, B_NOPE + B_VDIM)
        wkv = jnp.concatenate([wkv[:, :, :B_NOPE].reshape(B_KV_RANK, -1), wkv[:, :, B_NOPE:].reshape(B_KV_RANK, -1)],
                              axis=1).astype(BF16)
        norms = (a_q_norm[l][None, :], a_k_norm[l][None, :], b_q_norm[l][None, :], b_kv_norm[l][None, :])
        qa, qb, qd, ka, va, kb, vb, kd, vd = _prep(p, tabs, norms, wq, wkv, nb=nb, seq=seq, n_lat_rows=rl,
                                                   n_rows=r_all)

        attn = functools.partial(_attention, nb=nb, seq=seq, n_q_tiles=n_q_tiles)
        br_a = attn(qa, ka, va, group=A_HEADS // A_KV_HEADS, dk=HEAD_DIM, scale=scale_hd, name="attn_axial")
        br_b = attn(qb, kb, vb, group=1, dk=2 * B_NOPE, scale=scale_mla, name="attn_mla")
        br_d = attn(qd, kd, vd, group=D_HEADS // D_KV_HEADS, dk=HEAD_DIM, scale=scale_hd, sink=d_sink[l],
                    name="attn_window")
        br_c = _conv(p, c_conv_w[l], c_conv_b[l][None, :], c_ln_g[l][None, :], c_ln_b[l][None, :],
                     seq=seq, n_lat_rows=rl, n_rows=n_rows)

        w_brs = [w.astype(BF16) for w in (w_br_a[l], w_br_b[l], w_br_c[l], w_br_d[l])]
        acc = _merge((br_a, br_b, br_c, br_d), w_brs, p, n_rows=n_rows, d_model=d_model)
        x1 = _wo_ln(acc, w_o[l].astype(BF16), x_all, g1, ln1_g[l][None, :], ln1_b[l][None, :],
                    n_rows=n_rows, mod_row=mod_row(TOK), alpha=alpha)

        q = _mm(x1, peer_wq[l].astype(BF16), n_rows=n_rows, tm=MM_TM, tn=MM_TN, out_dtype=BF16,
                prologue="mod", mod=(s2, sh2), mod_row=mod_row(MM_TM), name="peer_query")
        idx, gsel = _peer_topk(q, peer_subkeys[l].astype(BF16), n_rows=n_rows)
        x_all = _peer_gather(idx.T, gsel, x1, (s2, sh2, g2), ln2_g[l][None, :], ln2_b[l][None, :],
                             peer_u[l], peer_v[l], n_rows=n_rows, mod_row=mod_row(PEER_TT), alpha=alpha)

    return x_all[:rl].reshape(nb, seq, d_model)
```

```python
import functools

import jax
import jax.numpy as jnp
from jax import lax
from jax.experimental import pallas as pl
from jax.experimental.pallas import tpu as pltpu

F32 = jnp.float32
BF16 = jnp.bfloat16

GRID_W = 64
HEAD_DIM = 128
ROPE_THETA = 10000.0
A_HEADS, A_KV_HEADS = 8, 2
B_HEADS, B_Q_RANK, B_KV_RANK, B_NOPE, B_ROPE, B_VDIM = 8, 512, 256, 128, 64, 128
B_QK = B_NOPE + B_ROPE
C_CH, CONV_K = 1024, 31
D_HEADS, D_KV_HEADS = 8, 2
WINDOW = 128
N_BRANCH = 4
PEER_HEADS, PEER_NKEYS, PEER_TOPK, PEER_QDIM = 8, 128, 16, 256
RMS_EPS = 1e-6
LN_EPS = 1e-5
NEG_INF = -1e30

LANES = 128
TOK = 256
MM_TM = 512
MM_TN = 512
PEER_TT = 128
CONV_HALO = 16
VMEM_LIMIT = 56 * 1024 * 1024

P_AQ, P_DQ, P_GLU_A, P_GLU_G, P_GATE = 0, 1024, 2048, 3072, 4096
P_KV = 12288
KV_W = 1536
P_BCQ = 13824
NP_COLS = 14336


def _cparams(*sem):
    return pltpu.CompilerParams(dimension_semantics=sem, vmem_limit_bytes=VMEM_LIMIT)


def _layer_norm_rows(z, g, b):
    mu = jnp.mean(z, axis=-1, keepdims=True)
    zc = z - mu
    var = jnp.mean(zc * zc, axis=-1, keepdims=True)
    return zc * lax.rsqrt(var + LN_EPS) * g + b


def _mm_body(*refs, prologue, has_bias):
    refs = list(refs)
    x_ref = refs.pop(0)
    s_ref = sh_ref = b_ref = None
    if prologue == "mod":
        s_ref, sh_ref = refs.pop(0), refs.pop(0)
    w_ref = refs.pop(0)
    if has_bias:
        b_ref = refs.pop(0)
    o_ref, h_scr = refs

    @pl.when(pl.program_id(1) == 0)
    def _():
        xv = x_ref[...].astype(F32)
        if prologue == "mod":
            xv = xv * (1.0 + s_ref[0]) + sh_ref[0]
        elif prologue == "silu":
            xv = xv * jax.nn.sigmoid(xv)
        h_scr[...] = xv.astype(BF16)

    acc = jnp.dot(h_scr[...], w_ref[...].astype(BF16), preferred_element_type=F32)
    if has_bias:
        acc = acc + b_ref[...]
    o_ref[...] = acc.astype(o_ref.dtype)


def _mm(x, w, *, n_rows, tm, tn, out_dtype, prologue="none", mod=None, mod_row=None, bias=None, name=None):
    k_dim = x.shape[1]
    n_dim = w.shape[1]
    in_specs = [pl.BlockSpec((tm, k_dim), lambda i, j: (i, 0))]
    args = [x]
    if prologue == "mod":
        spec = pl.BlockSpec((1, 1, k_dim), lambda i, j: (mod_row(i), 0, 0))
        in_specs += [spec, spec]
        args += list(mod)
    in_specs.append(pl.BlockSpec((k_dim, tn), lambda i, j: (0, j)))
    args.append(w)
    if bias is not None:
        in_specs.append(pl.BlockSpec((1, tn), lambda i, j: (0, j)))
        args.append(bias)
    return pl.pallas_call(
        functools.partial(_mm_body, prologue=prologue, has_bias=bias is not None),
        grid=(n_rows // tm, n_dim // tn),
        in_specs=in_specs,
        out_specs=pl.BlockSpec((tm, tn), lambda i, j: (i, j)),
        out_shape=jax.ShapeDtypeStruct((n_rows, n_dim), out_dtype),
        scratch_shapes=[pltpu.VMEM((tm, k_dim), BF16)],
        compiler_params=_cparams("parallel", "arbitrary"),
        name=name,
    )(*args)


def _prep_body(aq_ref, dq_ref, kv_ref, cq_ref, cosh_ref, sinh_ref, cosr_ref, sinr_ref,
               aqn_ref, akn_ref, bqn_ref, bkvn_ref, wq_ref, wkv_ref,
               qa_ref, qb_ref, qd_ref, ka_ref, va_ref, kb_ref, vb_ref, kd_ref, vd_ref):
    cosh, sinh = cosh_ref[...], sinh_ref[...]
    cosr, sinr = cosr_ref[...], sinr_ref[...]
    lane = lax.broadcasted_iota(jnp.int32, (TOK, LANES), 1)
    low32 = (lane % 64) < 32
    low64 = lane < 64

    def rope_head(xh):
        return xh * cosh + pltpu.roll(xh, 64, 1) * sinh

    def rope_r64(xc):
        sw = jnp.where(low32, pltpu.roll(xc, 96, 1), pltpu.roll(xc, 32, 1))
        return xc * cosr + sw * sinr

    def rms(xh, g):
        ms = jnp.mean(xh * xh, axis=-1, keepdims=True)
        return xh * lax.rsqrt(ms + RMS_EPS) * g

    aqn, akn = aqn_ref[...], akn_ref[...]
    for h in range(A_HEADS):
        sl = slice(h * HEAD_DIM, (h + 1) * HEAD_DIM)
        qa_ref[:, sl] = rope_head(rms(aq_ref[:, sl], aqn)).astype(BF16)
        qd_ref[:, sl] = rope_head(dq_ref[:, sl]).astype(BF16)
    for h in range(A_KV_HEADS):
        sl = slice(h * HEAD_DIM, (h + 1) * HEAD_DIM)
        ka_ref[h] = rope_head(rms(kv_ref[:, sl], akn)).astype(BF16)
        va_ref[h] = kv_ref[:, 256 + h * HEAD_DIM:256 + (h + 1) * HEAD_DIM].T.astype(BF16)
        kd_ref[h] = rope_head(kv_ref[:, 768 + h * HEAD_DIM:768 + (h + 1) * HEAD_DIM]).astype(BF16)
        vd_ref[h] = kv_ref[:, 1024 + h * HEAD_DIM:1024 + (h + 1) * HEAD_DIM].T.astype(BF16)

    ckv = rms(kv_ref[:, 512:768], bkvn_ref[...]).astype(BF16)
    kv_up = jnp.dot(ckv, wkv_ref[...], preferred_element_type=F32)
    kr = rope_r64(kv_ref[:, 1280:1408]).astype(BF16)
    for h in range(B_HEADS):
        kb_ref[h, :, 0:B_NOPE] = kv_up[:, h * B_NOPE:(h + 1) * B_NOPE].astype(BF16)
        kb_ref[h, :, B_NOPE:2 * B_NOPE] = kr
        vb_ref[h] = kv_up[:, 1024 + h * B_VDIM:1024 + (h + 1) * B_VDIM].T.astype(BF16)

    cq = rms(cq_ref[...], bqn_ref[...]).astype(BF16)
    q_up = jnp.dot(cq, wq_ref[...], preferred_element_type=F32)
    for c in range(B_HEADS // 2):
        rc = rope_r64(q_up[:, 1024 + c * LANES:1024 + (c + 1) * LANES])
        for half in range(2):
            h = 2 * c + half
            piece = rc if half == 0 else pltpu.roll(rc, 64, 1)
            qb_ref[:, h * 256:h * 256 + B_NOPE] = q_up[:, h * B_NOPE:(h + 1) * B_NOPE].astype(BF16)
            qb_ref[:, h * 256 + B_NOPE:(h + 1) * 256] = jnp.where(low64, piece, 0.0).astype(BF16)


def _prep(p, tabs, norms, wq, wkv, *, nb, seq, n_lat_rows, n_rows):
    nql = seq // TOK
    n_lat_tiles = n_lat_rows // TOK
    t_len = seq + TOK

    def colspec(width, off):
        return pl.BlockSpec((TOK, width), lambda t: (t, off // width))

    def tabspec():
        return pl.BlockSpec((TOK, LANES), lambda t: (jnp.where(t < n_lat_tiles, t % nql, nql), 0))

    def full(a):
        return pl.BlockSpec(a.shape, lambda t: (0,) * a.ndim)

    def kvspec(heads, d):
        return pl.BlockSpec(
            (None, heads, TOK, d),
            lambda t: (jnp.where(t < n_lat_tiles, t // nql, t - n_lat_tiles), 0,
                       jnp.where(t < n_lat_tiles, t % nql, nql), 0))

    def vtspec(heads):
        return pl.BlockSpec(
            (None, heads, None, HEAD_DIM, TOK),
            lambda t: (jnp.where(t < n_lat_tiles, t // nql, t - n_lat_tiles), 0,
                       jnp.where(t < n_lat_tiles, t % nql, nql), 0, 0))

    def rowspec(width):
        return pl.BlockSpec((TOK, width), lambda t: (t, 0))

    def kvshape(heads, d):
        return jax.ShapeDtypeStruct((nb, heads, t_len, d), BF16)

    def vtshape(heads):
        return jax.ShapeDtypeStruct((nb, heads, nql + 1, HEAD_DIM, TOK), BF16)

    return pl.pallas_call(
        _prep_body,
        grid=(n_rows // TOK,),
        in_specs=[colspec(1024, P_AQ), colspec(1024, P_DQ), colspec(KV_W, P_KV), colspec(512, P_BCQ),
                  tabspec(), tabspec(), tabspec(), tabspec(),
                  full(norms[0]), full(norms[1]), full(norms[2]), full(norms[3]), full(wq), full(wkv)],
        out_specs=[rowspec(1024), rowspec(2048), rowspec(1024),
                   kvspec(2, 128), vtspec(2), kvspec(8, 256), vtspec(8), kvspec(2, 128), vtspec(2)],
        out_shape=[jax.ShapeDtypeStruct((n_rows, 1024), BF16), jax.ShapeDtypeStruct((n_rows, 2048), BF16),
                   jax.ShapeDtypeStruct((n_rows, 1024), BF16),
                   kvshape(2, 128), vtshape(2), kvshape(8, 256), vtshape(8), kvshape(2, 128), vtshape(2)],
        compiler_params=_cparams("parallel"),
        name="qkv_prep",
    )(p, p, p, p, *tabs, *norms, wq, wkv)


LOG2E = 1.4426950408889634


def _softmax_steps(qs, ks, vts, c_exp, masks, carries):
    n = len(qs)
    ss = [lax.dot_general(ks[i], qs[i], (((1,), (1,)), ((), ())), preferred_element_type=F32) for i in range(n)]
    prs, heads = [], []
    for i in range(n):
        m, l, acc = carries[i]
        s = ss[i] if masks[i] is None else jnp.where(masks[i], ss[i], NEG_INF)
        m_new = jnp.maximum(m, jnp.max(s, axis=0, keepdims=True))
        alpha = jnp.exp2((m - m_new) * c_exp)
        pr = jnp.exp2((s - m_new) * c_exp)
        prs.append(pr.astype(BF16))
        heads.append((m_new, alpha * l + jnp.sum(pr, axis=0, keepdims=True), alpha * acc))
    return [(heads[i][0], heads[i][1], heads[i][2] + jnp.dot(vts[i], prs[i], preferred_element_type=F32))
            for i in range(n)]


def _attn_store(o_ref, g, carry):
    _, l, acc = carry
    o_ref[:, g * HEAD_DIM:(g + 1) * HEAD_DIM] = (acc * (1.0 / l)).T.astype(o_ref.dtype)


def _attn_dense_body(q_ref, k_ref, v_ref, o_ref, *, group, dk, scale, nql, shared_kv):
    qi = pl.program_id(2)
    c0 = jnp.where(qi >= nql, nql, 0)
    c_exp = scale * LOG2E

    def step(c, carry):
        off = pl.multiple_of(c * TOK, TOK)
        qs = [q_ref[:, g * dk:(g + 1) * dk] for g in range(group)]
        ks = [k_ref[pl.ds(off, TOK), :] if shared_kv else k_ref[g, pl.ds(off, TOK), :] for g in range(group)]
        vts = [v_ref[c] if shared_kv else v_ref[g, c] for g in range(group)]
        return tuple(_softmax_steps(qs, ks, vts, c_exp, [None] * group, list(carry)))

    init = tuple((jnp.full((1, TOK), NEG_INF, F32), jnp.zeros((1, TOK), F32), jnp.zeros((HEAD_DIM, TOK), F32))
                 for _ in range(group))
    res = lax.fori_loop(c0, nql + 1, step, init)
    for g in range(group):
        _attn_store(o_ref, g, res[g])


def _attn_win_body(sink_ref, q_ref, k_ref, v_ref, o_ref, *, group, scale, nql):
    kvh = pl.program_id(1)
    qi = pl.program_id(2)
    is_lat = qi < nql
    c_exp = scale * LOG2E
    kiota = lax.broadcasted_iota(jnp.int32, (TOK, TOK), 0)
    qpos = qi * TOK + lax.broadcasted_iota(jnp.int32, (TOK, TOK), 1)
    qs = [q_ref[:, g * HEAD_DIM:(g + 1) * HEAD_DIM] for g in range(group)]
    carries = [(jnp.full((1, TOK), sink_ref[kvh * group + g] * (1.0 / scale), F32), jnp.ones((1, TOK), F32),
                jnp.zeros((HEAD_DIM, TOK), F32)) for g in range(group)]
    for j in range(3):
        c = qi - 1 + j
        valid = jnp.logical_and(jnp.logical_and(c >= 0, c < nql), is_lat)
        cc = jnp.clip(c, 0, nql - 1)
        k = k_ref[pl.ds(pl.multiple_of(cc * TOK, TOK), TOK), :]
        mask = jnp.logical_and(jnp.abs(qpos - (c * TOK + kiota)) <= WINDOW, valid)
        carries = _softmax_steps(qs, [k] * group, [v_ref[cc]] * group, c_exp, [mask] * group, carries)
    carries = _softmax_steps(qs, [k_ref[nql * TOK:(nql + 1) * TOK, :]] * group, [v_ref[nql]] * group, c_exp,
                             [None] * group, carries)
    for g in range(group):
        _attn_store(o_ref, g, carries[g])


def _attention(q, k, v, *, nb, seq, n_q_tiles, group, dk, scale, sink=None, name=None):
    nql = seq // TOK
    t_len = k.shape[2]
    n_heads = q.shape[1] // dk
    shared_kv = k.shape[1] * group == n_heads
    kvh = n_heads // group

    def qrow(b, qi):
        return jnp.where(qi < nql, b * nql + qi, nb * nql + b)

    q_spec = pl.BlockSpec((TOK, group * dk), lambda b, h, qi: (qrow(b, qi), h))
    kv_heads = None if shared_kv else group
    k_spec = pl.BlockSpec((None, kv_heads, t_len, dk), lambda b, h, qi: (b, h, 0, 0))
    v_spec = pl.BlockSpec((None, kv_heads, nql + 1, HEAD_DIM, TOK), lambda b, h, qi: (b, h, 0, 0, 0))
    o_spec = pl.BlockSpec((TOK, group * HEAD_DIM), lambda b, h, qi: (qrow(b, qi), h))
    n_rows_out = nb * seq + (nb * TOK if n_q_tiles > nql else 0)
    if sink is None:
        body = functools.partial(_attn_dense_body, group=group, dk=dk, scale=scale, nql=nql, shared_kv=shared_kv)
        in_specs, args = [q_spec, k_spec, v_spec], (q, k, v)
    else:
        body = functools.partial(_attn_win_body, group=group, scale=scale, nql=nql)
        in_specs = [pl.BlockSpec(memory_space=pltpu.SMEM), q_spec, k_spec, v_spec]
        args = (sink, q, k, v)
    return pl.pallas_call(
        body,
        grid=(nb, kvh, n_q_tiles),
        in_specs=in_specs,
        out_specs=o_spec,
        out_shape=jax.ShapeDtypeStruct((n_rows_out, kvh * group * HEAD_DIM), BF16),
        compiler_params=_cparams("parallel", "parallel", "arbitrary"),
        name=name,
    )(*args)


def _conv_body(a_ref, g_ref, ap_ref, gp_ref, an_ref, gn_ref, w_ref, cb_ref, lg_ref, lb_ref, o_ref, u_scr, y_scr,
               *, nql, n_lat_tiles):
    t = pl.program_id(0)
    is_lat = t < n_lat_tiles
    i = t % nql
    has_prev = jnp.logical_and(is_lat, i > 0)
    has_next = jnp.logical_and(is_lat, i < nql - 1)
    up = ap_ref[...] * jax.nn.sigmoid(gp_ref[...])
    un = an_ref[...] * jax.nn.sigmoid(gn_ref[...])
    u_scr[0:CONV_HALO, :] = jnp.where(has_prev, up, 0.0)
    u_scr[CONV_HALO:CONV_HALO + TOK, :] = a_ref[...] * jax.nn.sigmoid(g_ref[...])
    u_scr[CONV_HALO + TOK:, :] = jnp.where(has_next, un, 0.0)
    base = CONV_HALO - CONV_K // 2
    for lc in range(C_CH // LANES):
        ls = slice(lc * LANES, (lc + 1) * LANES)
        acc = jnp.zeros((TOK, LANES), F32)
        for k in range(CONV_K):
            acc = acc + u_scr[base + k:base + k + TOK, ls] * w_ref[k:k + 1, ls]
        y_scr[:, ls] = acc + cb_ref[:, ls]
    y = _layer_norm_rows(y_scr[...], lg_ref[...], lb_ref[...])
    o_ref[...] = (y * jax.nn.sigmoid(y)).astype(o_ref.dtype)


def _conv(p, conv_w, conv_b, ln_g, ln_b, *, seq, n_lat_rows, n_rows):
    nql = seq // TOK
    hb = TOK // CONV_HALO
    last_hb = p.shape[0] // CONV_HALO - 1

    def cur(off):
        return pl.BlockSpec((TOK, C_CH), lambda t: (t, off // C_CH))

    def prev(off):
        return pl.BlockSpec((CONV_HALO, C_CH), lambda t: (jnp.maximum(t * hb - 1, 0), off // C_CH))

    def nxt(off):
        return pl.BlockSpec((CONV_HALO, C_CH), lambda t: (jnp.minimum((t + 1) * hb, last_hb), off // C_CH))

    def full(a):
        return pl.BlockSpec(a.shape, lambda t: (0,) * a.ndim)

    return pl.pallas_call(
        functools.partial(_conv_body, nql=nql, n_lat_tiles=n_lat_rows // TOK),
        grid=(n_rows // TOK,),
        in_specs=[cur(P_GLU_A), cur(P_GLU_G), prev(P_GLU_A), prev(P_GLU_G), nxt(P_GLU_A), nxt(P_GLU_G),
                  full(conv_w), full(conv_b), full(ln_g), full(ln_b)],
        out_specs=pl.BlockSpec((TOK, C_CH), lambda t: (t, 0)),
        out_shape=jax.ShapeDtypeStruct((n_rows, C_CH), BF16),
        scratch_shapes=[pltpu.VMEM((TOK + 2 * CONV_HALO, C_CH), F32), pltpu.VMEM((TOK, C_CH), F32)],
        compiler_params=_cparams("parallel"),
        name="conformer_conv",
    )(p, p, p, p, p, p, conv_w, conv_b, ln_g, ln_b)


def _merge_body(xa_ref, xb_ref, xc_ref, xd_ref, wa_ref, wb_ref, wc_ref, wd_ref,
                ga_ref, gb_ref, gc_ref, gd_ref, o_ref):
    acc = None
    for x_ref, w_ref, g_ref in ((xa_ref, wa_ref, ga_ref), (xb_ref, wb_ref, gb_ref),
                                (xc_ref, wc_ref, gc_ref), (xd_ref, wd_ref, gd_ref)):
        term = jax.nn.sigmoid(g_ref[...]) * jnp.dot(x_ref[...], w_ref[...], preferred_element_type=F32)
        acc = term if acc is None else acc + term
    o_ref[...] = acc.astype(o_ref.dtype)


def _merge(branches, w_brs, p, *, n_rows, d_model):
    tm, tn = MM_TM, MM_TN
    x_spec = pl.BlockSpec((tm, 1024), lambda i, j: (i, 0))
    w_spec = pl.BlockSpec((1024, tn), lambda i, j: (0, j))

    def gate_spec(n):
        return pl.BlockSpec((tm, tn), lambda i, j: (i, (P_GATE + n * d_model) // tn + j))

    return pl.pallas_call(
        _merge_body,
        grid=(n_rows // tm, d_model // tn),
        in_specs=[x_spec] * 4 + [w_spec] * 4 + [gate_spec(n) for n in range(N_BRANCH)],
        out_specs=pl.BlockSpec((tm, tn), lambda i, j: (i, j)),
        out_shape=jax.ShapeDtypeStruct((n_rows, d_model), BF16),
        compiler_params=_cparams("parallel", "arbitrary"),
        name="branch_merge",
    )(*branches, *w_brs, p, p, p, p)


def _wo_ln_body(acc_ref, w_ref, x_ref, g1_ref, lg_ref, lb_ref, o_ref, *, alpha):
    mix = jnp.dot(acc_ref[...], w_ref[...], preferred_element_type=F32)
    z = alpha * x_ref[...] + g1_ref[0] * mix
    o_ref[...] = _layer_norm_rows(z, lg_ref[...], lb_ref[...])


def _wo_ln(acc, w_o, x_all, g1, ln_g, ln_b, *, n_rows, mod_row, alpha):
    d_model = w_o.shape[0]
    row = pl.BlockSpec((TOK, d_model), lambda i: (i, 0))
    vec = pl.BlockSpec((1, d_model), lambda i: (0, 0))
    return pl.pallas_call(
        functools.partial(_wo_ln_body, alpha=alpha),
        grid=(n_rows // TOK,),
        in_specs=[row, pl.BlockSpec(w_o.shape, lambda i: (0, 0)), row,
                  pl.BlockSpec((1, 1, d_model), lambda i: (mod_row(i), 0, 0)), vec, vec],
        out_specs=row,
        out_shape=jax.ShapeDtypeStruct((n_rows, d_model), F32),
        compiler_params=_cparams("parallel"),
        name="wo_deepnorm",
    )(acc, w_o, x_all, g1, ln_g, ln_b)


def _topk_rows(s, n_out, rowi):
    n = s.shape[0]
    vals, poss = [], []
    for _ in range(n_out):
        m = jnp.max(s, axis=0, keepdims=True)
        pos = jnp.min(jnp.where(s == m, rowi, n), axis=0, keepdims=True)
        s = jnp.where(rowi == pos, -jnp.inf, s)
        vals.append(m)
        poss.append(pos)
    return vals, poss


def _peer_topk_body(q_ref, sk_ref, idx_ref, g_ref, ts_scr, ti_scr, cs_scr, ci_scr):
    k = PEER_TOPK
    row_keys = lax.broadcasted_iota(jnp.int32, (PEER_NKEYS, TOK), 0)
    row_cand = lax.broadcasted_iota(jnp.int32, (k * k, TOK), 0)
    for h in range(PEER_HEADS):
        for part in range(2):
            c = h * 2 + part
            s_t = lax.dot_general(sk_ref[h, part], q_ref[:, c * LANES:(c + 1) * LANES],
                                  (((1,), (1,)), ((), ())), preferred_element_type=F32)
            vals, poss = _topk_rows(s_t, k, row_keys)
            for a in range(k):
                ts_scr[part, a:a + 1, :] = vals[a]
                ti_scr[part, a:a + 1, :] = poss[a]
        s2 = ts_scr[1]
        i2 = ti_scr[1]
        for a in range(k):
            cs_scr[a * k:(a + 1) * k, :] = ts_scr[0, a:a + 1, :] + s2
            ci_scr[a * k:(a + 1) * k, :] = ti_scr[0, a:a + 1, :] * PEER_NKEYS + i2
        vals, poss = _topk_rows(cs_scr[...], k, row_cand)
        cand_i = ci_scr[...]
        exps = [jnp.exp(v - vals[0]) for v in vals]
        denom = exps[0]
        for e in exps[1:]:
            denom = denom + e
        inv = 1.0 / denom
        for a in range(k):
            eid = jnp.sum(jnp.where(row_cand == poss[a], cand_i, 0), axis=0, keepdims=True)
            idx_ref[h * k + a:h * k + a + 1, :] = eid
            g_ref[h * k + a:h * k + a + 1, :] = exps[a] * inv


def _peer_topk(q, subkeys, *, n_rows):
    k = PEER_TOPK
    n_sel = PEER_HEADS * k
    return pl.pallas_call(
        _peer_topk_body,
        grid=(n_rows // TOK,),
        in_specs=[pl.BlockSpec((TOK, q.shape[1]), lambda t: (t, 0)),
                  pl.BlockSpec(subkeys.shape, lambda t: (0, 0, 0, 0))],
        out_specs=[pl.BlockSpec((n_sel, TOK), lambda t: (0, t)), pl.BlockSpec((n_sel, TOK), lambda t: (0, t))],
        out_shape=[jax.ShapeDtypeStruct((n_sel, n_rows), jnp.int32), jax.ShapeDtypeStruct((n_sel, n_rows), F32)],
        scratch_shapes=[pltpu.VMEM((2, k, TOK), F32), pltpu.VMEM((2, k, TOK), jnp.int32),
                        pltpu.VMEM((k * k, TOK), F32), pltpu.VMEM((k * k, TOK), jnp.int32)],
        compiler_params=_cparams("parallel"),
        name="peer_topk",
    )(q, subkeys)


HI16 = -65536


def _peer_pack_body(u_ref, v_ref, o_ref):
    ub = lax.bitcast_convert_type(u_ref[...].astype(BF16).astype(F32), jnp.int32)
    vb = lax.bitcast_convert_type(v_ref[...].astype(BF16).astype(F32), jnp.int32)
    o_ref[...] = lax.shift_right_logical(ub, 16) | (vb & HI16)


def _peer_pack(u, v):
    n_exp, d_model = u.shape
    spec = pl.BlockSpec((MM_TM, d_model), lambda i: (i, 0))
    return pl.pallas_call(
        _peer_pack_body,
        grid=(n_exp // MM_TM,),
        in_specs=[spec, spec],
        out_specs=spec,
        out_shape=jax.ShapeDtypeStruct((n_exp, d_model), jnp.int32),
        compiler_params=_cparams("parallel"),
        name="peer_pack",
    )(u, v)


def _peer_gather_body(idx_ref, g_ref, x_ref, s2_ref, sh2_ref, g2_ref, lg_ref, lb_ref, uv_hbm, o_ref,
                      buf0, buf1, h_scr, y_scr, sem, *, alpha):
    n_sel, d_model = buf0.shape
    sub = 8
    bufs = (buf0, buf1)
    h_scr[...] = x_ref[...] * (1.0 + s2_ref[0]) + sh2_ref[0]

    def issue(t, slot):
        for k in range(n_sel):
            pltpu.make_async_copy(uv_hbm.at[idx_ref[t, k]], bufs[slot].at[pl.ds(k, 1)],
                                  sem.at[slot]).start(priority=k % 2)

    def wait(slot):
        pltpu.make_async_copy(uv_hbm.at[pl.ds(0, n_sel), 0], bufs[slot], sem.at[slot]).wait()

    lane = lax.broadcasted_iota(jnp.int32, (n_sel, PEER_TT), 1)

    def token(t, slot):
        wait(slot)
        issue(jnp.minimum(t + 1, PEER_TT - 1), 1 - slot)
        buf = bufs[slot]
        h = h_scr[pl.ds(t, 1), :]
        gcol = jnp.sum(jnp.where(lane == t, g_ref[...], 0.0), axis=1, keepdims=True)
        yacc = jnp.zeros((sub, d_model), F32)
        for j in range(n_sel // sub):
            w = buf[j * sub:(j + 1) * sub, :]
            u = lax.bitcast_convert_type(lax.shift_left(w, 16), F32)
            a = jnp.sum(u * h, axis=1, keepdims=True)
            act = 0.5 * a * (1.0 + lax.erf(a * 0.7071067811865476))
            v = lax.bitcast_convert_type(w & HI16, F32)
            yacc = yacc + v * (act * gcol[j * sub:(j + 1) * sub])
        y_scr[pl.ds(t, 1), :] = jnp.sum(yacc, axis=0, keepdims=True)

    def pair(i, carry):
        token(2 * i, 0)
        token(2 * i + 1, 1)
        return carry

    issue(0, 0)
    lax.fori_loop(0, PEER_TT // 2, pair, 0)
    wait(0)
    z = alpha * x_ref[...] + g2_ref[0] * y_scr[...]
    o_ref[...] = _layer_norm_rows(z, lg_ref[...], lb_ref[...])


def _peer_gather(idx_t, g_t, x1, mods, ln_g, ln_b, uv, *, n_rows, mod_row, alpha):
    d_model = x1.shape[1]
    n_sel = idx_t.shape[1]
    row = pl.BlockSpec((PEER_TT, d_model), lambda t: (t, 0))
    vec = pl.BlockSpec((1, d_model), lambda t: (0, 0))
    mod = pl.BlockSpec((1, 1, d_model), lambda t: (mod_row(t), 0, 0))
    return pl.pallas_call(
        functools.partial(_peer_gather_body, alpha=alpha),
        grid=(n_rows // PEER_TT,),
        in_specs=[pl.BlockSpec((PEER_TT, n_sel), lambda t: (t, 0), memory_space=pltpu.SMEM),
                  pl.BlockSpec((n_sel, PEER_TT), lambda t: (0, t)),
                  row, mod, mod, mod, vec, vec, pl.BlockSpec(memory_space=pl.ANY)],
        out_specs=row,
        out_shape=jax.ShapeDtypeStruct((n_rows, d_model), F32),
        scratch_shapes=[pltpu.VMEM((n_sel, d_model), jnp.int32), pltpu.VMEM((n_sel, d_model), jnp.int32),
                        pltpu.VMEM((PEER_TT, d_model), F32), pltpu.VMEM((PEER_TT, d_model), F32),
                        pltpu.SemaphoreType.DMA((2,))],
        compiler_params=_cparams("arbitrary"),
        name="peer_gather",
    )(idx_t, g_t, x1, *mods, ln_g, ln_b, uv)


def _rope_tables(seq):
    rows = seq // GRID_W
    row = jnp.repeat(jnp.arange(rows, dtype=F32), GRID_W)
    col = jnp.tile(jnp.arange(GRID_W, dtype=F32), rows)

    def tables(dim):
        half = dim // 2
        freq = ROPE_THETA ** (-jnp.arange(0, half, 2, dtype=F32) / half)
        ang = jnp.concatenate([row[:, None] * freq, col[:, None] * freq], axis=-1)
        return jnp.cos(ang), jnp.sin(ang)

    def with_identity(cos_t, sin_t):
        return (jnp.concatenate([cos_t, jnp.ones((TOK, LANES), F32)], axis=0),
                jnp.concatenate([sin_t, jnp.zeros((TOK, LANES), F32)], axis=0))

    cos_h, sin_h = tables(HEAD_DIM)
    cos_r, sin_r = tables(B_ROPE)
    cosh, sinh = with_identity(jnp.concatenate([cos_h, cos_h], -1), jnp.concatenate([-sin_h, sin_h], -1))
    cosr, sinr = with_identity(jnp.tile(cos_r, (1, 4)), jnp.tile(jnp.concatenate([-sin_r, sin_r], -1), (1, 2)))
    return cosh, sinh, cosr, sinr


def _permute_w_in(w):
    d_model = w.shape[0]
    a_w, a_kv_w = A_HEADS * HEAD_DIM, A_KV_HEADS * HEAD_DIM
    d_w, d_kv_w = D_HEADS * HEAD_DIM, D_KV_HEADS * HEAD_DIM
    off_ak = 0
    off_av = off_ak + a_kv_w
    off_bckv = off_av + a_kv_w
    off_bkr = off_bckv + B_KV_RANK
    off_dk = off_bkr + B_ROPE
    off_dv = off_dk + d_kv_w
    off_aq = off_dv + d_kv_w
    off_bcq = off_aq + a_w
    off_dq = off_bcq + B_Q_RANK
    off_glu = off_dq + d_w
    off_gate = off_glu + 2 * C_CH

    def seg(off, n):
        return w[:, off:off + n]

    out = jnp.concatenate(
        [seg(off_aq, a_w), seg(off_dq, d_w), seg(off_glu, 2 * C_CH), seg(off_gate, N_BRANCH * d_model),
         seg(off_ak, a_kv_w), seg(off_av, a_kv_w), seg(off_bckv, B_KV_RANK), seg(off_dk, d_kv_w), seg(off_dv, d_kv_w),
         seg(off_bkr, B_ROPE), jnp.zeros((d_model, KV_W - 1280 - B_ROPE), w.dtype), seg(off_bcq, B_Q_RANK)], axis=1)
    assert out.shape[1] == NP_COLS
    return out.astype(BF16)


def kernel(x, c, ctx, c_ctx, w_ada, b_ada, w_in, a_q_norm, a_k_norm, b_q_norm, b_w_q_up, b_kv_norm, b_w_kv_up,
           c_conv_w, c_conv_b, c_ln_g, c_ln_b, d_sink, w_br_a, w_br_b, w_br_c, w_br_d, w_o,
           ln1_g, ln1_b, ln2_g, ln2_b, peer_wq, peer_subkeys, peer_u, peer_v):
    nb, seq, d_model = x.shape
    depth = w_ada.shape[0]
    assert ctx.shape[1] == TOK and seq % MM_TM == 0 and (nb * TOK) % MM_TM == 0
    assert d_model == 2048 and P_GATE + N_BRANCH * d_model == P_KV
    rl, rc = nb * seq, nb * TOK
    r_all = rl + rc
    nql = seq // TOK
    alpha = (2 * depth) ** 0.25
    scale_hd = HEAD_DIM ** -0.5
    scale_mla = B_QK ** -0.5

    def mod_row(tile):
        return lambda i: jnp.where(i < rl // tile, i // (seq // tile), nb)

    tabs = _rope_tables(seq)
    x_all = jnp.concatenate([x.reshape(rl, d_model), ctx.reshape(rc, d_model)], axis=0)
    n_mod = 16
    cvec = jnp.concatenate([c, c_ctx[None, :], jnp.zeros((n_mod - nb - 1, d_model), F32)], axis=0)

    for l in range(depth):
        last = l == depth - 1
        n_rows = rl if last else r_all
        n_q_tiles = nql if last else nql + 1

        mod = _mm(cvec, w_ada[l], n_rows=n_mod, tm=n_mod, tn=MM_TN, out_dtype=F32, prologue="silu",
                  bias=b_ada[l][None, :], name="adaln")
        sh1, s1, g1, sh2, s2, g2 = [mod[:, i * d_model:(i + 1) * d_model].reshape(n_mod, 1, d_model)
                                    for i in range(6)]

        p = _mm(x_all, _permute_w_in(w_in[l]), n_rows=r_all, tm=MM_TM, tn=MM_TN, out_dtype=F32,
                prologue="mod", mod=(s1, sh1), mod_row=mod_row(MM_TM), name="in_proj")

        wq = b_w_q_up[l].reshape(B_Q_RANK, B_HEADS, B_QK)
        wq = jnp.concatenate([wq[:, :, :B_NOPE].reshape(B_Q_RANK, -1), wq[:, :, B_NOPE:].reshape(B_Q_RANK, -1)],
                             axis=1).astype(BF16)
        wkv = b_w_kv_up[l].reshape(B_KV_RANK, B_HEADS, B_NOPE + B_VDIM)
        wkv = jnp.concatenate([wkv[:, :, :B_NOPE].reshape(B_KV_RANK, -1), wkv[:, :, B_NOPE:].reshape(B_KV_RANK, -1)],
                              axis=1).astype(BF16)
        norms = (a_q_norm[l][None, :], a_k_norm[l][None, :], b_q_norm[l][None, :], b_kv_norm[l][None, :])
        qa, qb, qd, ka, va, kb, vb, kd, vd = _prep(p, tabs, norms, wq, wkv, nb=nb, seq=seq, n_lat_rows=rl,
                                                   n_rows=r_all)

        attn = functools.partial(_attention, nb=nb, seq=seq, n_q_tiles=n_q_tiles)
        br_a = attn(qa, ka, va, group=A_HEADS // A_KV_HEADS, dk=HEAD_DIM, scale=scale_hd, name="attn_axial")
        br_b = attn(qb, kb, vb, group=4, dk=2 * B_NOPE, scale=scale_mla, name="attn_mla")
        br_d = attn(qd, kd, vd, group=D_HEADS // D_KV_HEADS, dk=HEAD_DIM, scale=scale_hd, sink=d_sink[l],
                    name="attn_window")
        br_c = _conv(p, c_conv_w[l], c_conv_b[l][None, :], c_ln_g[l][None, :], c_ln_b[l][None, :],
                     seq=seq, n_lat_rows=rl, n_rows=n_rows)

        w_brs = [w.astype(BF16) for w in (w_br_a[l], w_br_b[l], w_br_c[l], w_br_d[l])]
        acc = _merge((br_a, br_b, br_c, br_d), w_brs, p, n_rows=n_rows, d_model=d_model)
        x1 = _wo_ln(acc, w_o[l].astype(BF16), x_all, g1, ln1_g[l][None, :], ln1_b[l][None, :],
                    n_rows=n_rows, mod_row=mod_row(TOK), alpha=alpha)

        q = _mm(x1, peer_wq[l].astype(BF16), n_rows=n_rows, tm=MM_TM, tn=MM_TN, out_dtype=BF16,
                prologue="mod", mod=(s2, sh2), mod_row=mod_row(MM_TM), name="peer_query")
        idx, gsel = _peer_topk(q, peer_subkeys[l].astype(BF16), n_rows=n_rows)
        x_all = _peer_gather(idx.T, gsel, x1, (s2, sh2, g2), ln2_g[l][None, :], ln2_b[l][None, :],
                             _peer_pack(peer_u[l], peer_v[l])[:, None, :], n_rows=n_rows,
                             mod_row=mod_row(PEER_TT), alpha=alpha)

    return x_all[:rl].reshape(nb, seq, d_model)
```

```python
import functools

import jax
import jax.numpy as jnp
from jax import lax
from jax.experimental import pallas as pl
from jax.experimental.pallas import tpu as pltpu

F32 = jnp.float32
BF16 = jnp.bfloat16

GRID_W = 64
HEAD_DIM = 128
ROPE_THETA = 10000.0
A_HEADS, A_KV_HEADS = 8, 2
B_HEADS, B_Q_RANK, B_KV_RANK, B_NOPE, B_ROPE, B_VDIM = 8, 512, 256, 128, 64, 128
B_QK = B_NOPE + B_ROPE
C_CH, CONV_K = 1024, 31
D_HEADS, D_KV_HEADS = 8, 2
WINDOW = 128
N_BRANCH = 4
PEER_HEADS, PEER_NKEYS, PEER_TOPK, PEER_QDIM = 8, 128, 16, 256
RMS_EPS = 1e-6
LN_EPS = 1e-5
NEG_INF = -1e30

LANES = 128
TOK = 256
MM_TM = 1024
MM_TN = 1024
MERGE_TM = 512
MERGE_TN = 512
PACK_TM = 512
ADA_TN = 512
PEER_TT = 128
PEER_NBUF = 8
PEER_AHEAD = 5
CONV_HALO = 16
VMEM_LIMIT = 56 * 1024 * 1024

P_AQ, P_DQ, P_GLU_A, P_GLU_G, P_GATE = 0, 1024, 2048, 3072, 4096
P_KV = 12288
KV_W = 1536
P_BCQ = 13824
NP_COLS = 14336


def _cparams(*sem):
    return pltpu.CompilerParams(dimension_semantics=sem, vmem_limit_bytes=VMEM_LIMIT)


def _layer_norm_rows(z, g, b):
    mu = jnp.mean(z, axis=-1, keepdims=True)
    zc = z - mu
    var = jnp.mean(zc * zc, axis=-1, keepdims=True)
    return zc * lax.rsqrt(var + LN_EPS) * g + b


def _mm_body(*refs, prologue, has_bias):
    refs = list(refs)
    x_ref = refs.pop(0)
    s_ref = sh_ref = b_ref = None
    if prologue == "mod":
        s_ref, sh_ref = refs.pop(0), refs.pop(0)
    w_ref = refs.pop(0)
    if has_bias:
        b_ref = refs.pop(0)
    o_ref, h_scr = refs

    @pl.when(pl.program_id(1) == 0)
    def _():
        xv = x_ref[...].astype(F32)
        if prologue == "mod":
            xv = xv * (1.0 + s_ref[0]) + sh_ref[0]
        elif prologue == "silu":
            xv = xv * jax.nn.sigmoid(xv)
        h_scr[...] = xv.astype(BF16)

    acc = jnp.dot(h_scr[...], w_ref[...].astype(BF16), preferred_element_type=F32)
    if has_bias:
        acc = acc + b_ref[...]
    if len(o_ref.shape) == 3:
        for c in range(o_ref.shape[0]):
            o_ref[c] = acc[:, c * LANES:(c + 1) * LANES].astype(o_ref.dtype)
    else:
        o_ref[...] = acc.astype(o_ref.dtype)


def _mm(x, w, *, n_rows, tm, tn, out_dtype, prologue="none", mod=None, mod_row=None, bias=None,
        split_cols=False, name=None):
    k_dim = x.shape[1]
    n_dim = w.shape[1]
    if split_cols:
        out_spec = pl.BlockSpec((tn // LANES, tm, LANES), lambda i, j: (j, i, 0))
        out_shape = jax.ShapeDtypeStruct((n_dim // LANES, n_rows, LANES), out_dtype)
    else:
        out_spec = pl.BlockSpec((tm, tn), lambda i, j: (i, j))
        out_shape = jax.ShapeDtypeStruct((n_rows, n_dim), out_dtype)
    in_specs = [pl.BlockSpec((tm, k_dim), lambda i, j: (i, 0))]
    args = [x]
    if prologue == "mod":
        spec = pl.BlockSpec((1, 1, k_dim), lambda i, j: (mod_row(i), 0, 0))
        in_specs += [spec, spec]
        args += list(mod)
    in_specs.append(pl.BlockSpec((k_dim, tn), lambda i, j: (0, j)))
    args.append(w)
    if bias is not None:
        in_specs.append(pl.BlockSpec((1, tn), lambda i, j: (0, j)))
        args.append(bias)
    return pl.pallas_call(
        functools.partial(_mm_body, prologue=prologue, has_bias=bias is not None),
        grid=(n_rows // tm, n_dim // tn),
        in_specs=in_specs,
        out_specs=out_spec,
        out_shape=out_shape,
        scratch_shapes=[pltpu.VMEM((tm, k_dim), BF16)],
        compiler_params=_cparams("parallel", "arbitrary"),
        name=name,
    )(*args)


def _prep_body(aq_ref, dq_ref, kv_ref, cq_ref, cosh_ref, sinh_ref, cosr_ref, sinr_ref,
               aqn_ref, akn_ref, bqn_ref, bkvn_ref, wq_ref, wkv_ref,
               qa_ref, qb_ref, qd_ref, ka_ref, va_ref, kb_ref, vb_ref, kd_ref, vd_ref):
    cosh, sinh = cosh_ref[...], sinh_ref[...]
    cosr, sinr = cosr_ref[...], sinr_ref[...]
    lane = lax.broadcasted_iota(jnp.int32, (TOK, LANES), 1)
    low32 = (lane % 64) < 32
    low64 = lane < 64

    def rope_head(xh):
        return xh * cosh + pltpu.roll(xh, 64, 1) * sinh

    def rope_r64(xc):
        sw = jnp.where(low32, pltpu.roll(xc, 96, 1), pltpu.roll(xc, 32, 1))
        return xc * cosr + sw * sinr

    def rms(xh, g):
        ms = jnp.mean(xh * xh, axis=-1, keepdims=True)
        return xh * lax.rsqrt(ms + RMS_EPS) * g

    aqn, akn = aqn_ref[...], akn_ref[...]
    for h in range(A_HEADS):
        sl = slice(h * HEAD_DIM, (h + 1) * HEAD_DIM)
        qa_ref[:, sl] = rope_head(rms(aq_ref[:, sl], aqn)).astype(BF16)
        qd_ref[:, sl] = rope_head(dq_ref[:, sl]).astype(BF16)
    for h in range(A_KV_HEADS):
        sl = slice(h * HEAD_DIM, (h + 1) * HEAD_DIM)
        ka_ref[h] = rope_head(rms(kv_ref[:, sl], akn)).astype(BF16)
        va_ref[h] = kv_ref[:, 256 + h * HEAD_DIM:256 + (h + 1) * HEAD_DIM].T.astype(BF16)
        kd_ref[h] = rope_head(kv_ref[:, 768 + h * HEAD_DIM:768 + (h + 1) * HEAD_DIM]).astype(BF16)
        vd_ref[h] = kv_ref[:, 1024 + h * HEAD_DIM:1024 + (h + 1) * HEAD_DIM].T.astype(BF16)

    ckv = rms(kv_ref[:, 512:768], bkvn_ref[...]).astype(BF16)
    kv_up = jnp.dot(ckv, wkv_ref[...], preferred_element_type=F32)
    kr = rope_r64(kv_ref[:, 1280:1408]).astype(BF16)
    for h in range(B_HEADS):
        kb_ref[h, :, 0:B_NOPE] = kv_up[:, h * B_NOPE:(h + 1) * B_NOPE].astype(BF16)
        kb_ref[h, :, B_NOPE:2 * B_NOPE] = kr
        vb_ref[h] = kv_up[:, 1024 + h * B_VDIM:1024 + (h + 1) * B_VDIM].T.astype(BF16)

    cq = rms(cq_ref[...], bqn_ref[...]).astype(BF16)
    q_up = jnp.dot(cq, wq_ref[...], preferred_element_type=F32)
    for c in range(B_HEADS // 2):
        rc = rope_r64(q_up[:, 1024 + c * LANES:1024 + (c + 1) * LANES])
        for half in range(2):
            h = 2 * c + half
            piece = rc if half == 0 else pltpu.roll(rc, 64, 1)
            qb_ref[:, h * 256:h * 256 + B_NOPE] = q_up[:, h * B_NOPE:(h + 1) * B_NOPE].astype(BF16)
            qb_ref[:, h * 256 + B_NOPE:(h + 1) * 256] = jnp.where(low64, piece, 0.0).astype(BF16)


def _prep(p, tabs, norms, wq, wkv, *, nb, seq, n_lat_rows, n_rows):
    nql = seq // TOK
    n_lat_tiles = n_lat_rows // TOK
    t_len = seq + TOK

    def colspec(width, off):
        return pl.BlockSpec((TOK, width), lambda t: (t, off // width))

    def tabspec():
        return pl.BlockSpec((TOK, LANES), lambda t: (jnp.where(t < n_lat_tiles, t % nql, nql), 0))

    def full(a):
        return pl.BlockSpec(a.shape, lambda t: (0,) * a.ndim)

    def kvspec(heads, d):
        return pl.BlockSpec(
            (None, heads, TOK, d),
            lambda t: (jnp.where(t < n_lat_tiles, t // nql, t - n_lat_tiles), 0,
                       jnp.where(t < n_lat_tiles, t % nql, nql), 0))

    def vtspec(heads):
        return pl.BlockSpec(
            (None, heads, None, HEAD_DIM, TOK),
            lambda t: (jnp.where(t < n_lat_tiles, t // nql, t - n_lat_tiles), 0,
                       jnp.where(t < n_lat_tiles, t % nql, nql), 0, 0))

    def rowspec(width):
        return pl.BlockSpec((TOK, width), lambda t: (t, 0))

    def kvshape(heads, d):
        return jax.ShapeDtypeStruct((nb, heads, t_len, d), BF16)

    def vtshape(heads):
        return jax.ShapeDtypeStruct((nb, heads, nql + 1, HEAD_DIM, TOK), BF16)

    return pl.pallas_call(
        _prep_body,
        grid=(n_rows // TOK,),
        in_specs=[colspec(1024, P_AQ), colspec(1024, P_DQ), colspec(KV_W, P_KV), colspec(512, P_BCQ),
                  tabspec(), tabspec(), tabspec(), tabspec(),
                  full(norms[0]), full(norms[1]), full(norms[2]), full(norms[3]), full(wq), full(wkv)],
        out_specs=[rowspec(1024), rowspec(2048), rowspec(1024),
                   kvspec(2, 128), vtspec(2), kvspec(8, 256), vtspec(8), kvspec(2, 128), vtspec(2)],
        out_shape=[jax.ShapeDtypeStruct((n_rows, 1024), BF16), jax.ShapeDtypeStruct((n_rows, 2048), BF16),
                   jax.ShapeDtypeStruct((n_rows, 1024), BF16),
                   kvshape(2, 128), vtshape(2), kvshape(8, 256), vtshape(8), kvshape(2, 128), vtshape(2)],
        compiler_params=_cparams("parallel"),
        name="qkv_prep",
    )(p, p, p, p, *tabs, *norms, wq, wkv)


LOG2E = 1.4426950408889634


def _softmax_steps(qs, ks, vts, c_exp, masks, carries):
    n = len(qs)
    ss = [lax.dot_general(ks[i], qs[i], (((1,), (1,)), ((), ())), preferred_element_type=F32) for i in range(n)]
    prs, heads = [], []
    for i in range(n):
        m, l, acc = carries[i]
        s = ss[i] if masks[i] is None else jnp.where(masks[i], ss[i], NEG_INF)
        m_new = jnp.maximum(m, jnp.max(s, axis=0, keepdims=True))
        alpha = jnp.exp2((m - m_new) * c_exp)
        pr = jnp.exp2((s - m_new) * c_exp)
        prs.append(pr.astype(BF16))
        heads.append((m_new, alpha * l + jnp.sum(pr, axis=0, keepdims=True), alpha * acc))
    return [(heads[i][0], heads[i][1], heads[i][2] + jnp.dot(vts[i], prs[i], preferred_element_type=F32))
            for i in range(n)]


def _attn_store(o_ref, g, carry):
    _, l, acc = carry
    o_ref[:, g * HEAD_DIM:(g + 1) * HEAD_DIM] = (acc * (1.0 / l)).T.astype(o_ref.dtype)


def _attn_dense_body(q_ref, k_ref, v_ref, o_ref, *, group, dk, scale, nql, shared_kv):
    qi = pl.program_id(2)
    c0 = jnp.where(qi >= nql, nql, 0)
    c_exp = scale * LOG2E

    def step(c, carry):
        off = pl.multiple_of(c * TOK, TOK)
        qs = [q_ref[:, g * dk:(g + 1) * dk] for g in range(group)]
        ks = [k_ref[pl.ds(off, TOK), :] if shared_kv else k_ref[g, pl.ds(off, TOK), :] for g in range(group)]
        vts = [v_ref[c] if shared_kv else v_ref[g, c] for g in range(group)]
        return tuple(_softmax_steps(qs, ks, vts, c_exp, [None] * group, list(carry)))

    init = tuple((jnp.full((1, TOK), NEG_INF, F32), jnp.zeros((1, TOK), F32), jnp.zeros((HEAD_DIM, TOK), F32))
                 for _ in range(group))
    res = lax.fori_loop(c0, nql + 1, step, init)
    for g in range(group):
        _attn_store(o_ref, g, res[g])


def _attn_win_body(sink_ref, q_ref, k_ref, v_ref, o_ref, *, group, scale, nql):
    kvh = pl.program_id(1)
    qi = pl.program_id(2)
    is_lat = qi < nql
    c_exp = scale * LOG2E
    kiota = lax.broadcasted_iota(jnp.int32, (TOK, TOK), 0)
    qpos = qi * TOK + lax.broadcasted_iota(jnp.int32, (TOK, TOK), 1)
    qs = [q_ref[:, g * HEAD_DIM:(g + 1) * HEAD_DIM] for g in range(group)]
    carries = [(jnp.full((1, TOK), sink_ref[kvh * group + g] * (1.0 / scale), F32), jnp.ones((1, TOK), F32),
                jnp.zeros((HEAD_DIM, TOK), F32)) for g in range(group)]
    for j in range(3):
        c = qi - 1 + j
        valid = jnp.logical_and(jnp.logical_and(c >= 0, c < nql), is_lat)
        cc = jnp.clip(c, 0, nql - 1)
        k = k_ref[pl.ds(pl.multiple_of(cc * TOK, TOK), TOK), :]
        mask = jnp.logical_and(jnp.abs(qpos - (c * TOK + kiota)) <= WINDOW, valid)
        carries = _softmax_steps(qs, [k] * group, [v_ref[cc]] * group, c_exp, [mask] * group, carries)
    carries = _softmax_steps(qs, [k_ref[nql * TOK:(nql + 1) * TOK, :]] * group, [v_ref[nql]] * group, c_exp,
                             [None] * group, carries)
    for g in range(group):
        _attn_store(o_ref, g, carries[g])


def _attention(q, k, v, *, nb, seq, n_q_tiles, group, dk, scale, sink=None, name=None):
    nql = seq // TOK
    t_len = k.shape[2]
    n_heads = q.shape[1] // dk
    shared_kv = k.shape[1] * group == n_heads
    kvh = n_heads // group

    def qrow(b, qi):
        return jnp.where(qi < nql, b * nql + qi, nb * nql + b)

    q_spec = pl.BlockSpec((TOK, group * dk), lambda b, h, qi: (qrow(b, qi), h))
    kv_heads = None if shared_kv else group
    k_spec = pl.BlockSpec((None, kv_heads, t_len, dk), lambda b, h, qi: (b, h, 0, 0))
    v_spec = pl.BlockSpec((None, kv_heads, nql + 1, HEAD_DIM, TOK), lambda b, h, qi: (b, h, 0, 0, 0))
    o_spec = pl.BlockSpec((TOK, group * HEAD_DIM), lambda b, h, qi: (qrow(b, qi), h))
    n_rows_out = nb * seq + (nb * TOK if n_q_tiles > nql else 0)
    if sink is None:
        body = functools.partial(_attn_dense_body, group=group, dk=dk, scale=scale, nql=nql, shared_kv=shared_kv)
        in_specs, args = [q_spec, k_spec, v_spec], (q, k, v)
    else:
        body = functools.partial(_attn_win_body, group=group, scale=scale, nql=nql)
        in_specs = [pl.BlockSpec(memory_space=pltpu.SMEM), q_spec, k_spec, v_spec]
        args = (sink, q, k, v)
    return pl.pallas_call(
        body,
        grid=(nb, kvh, n_q_tiles),
        in_specs=in_specs,
        out_specs=o_spec,
        out_shape=jax.ShapeDtypeStruct((n_rows_out, kvh * group * HEAD_DIM), BF16),
        compiler_params=_cparams("parallel", "parallel", "arbitrary"),
        name=name,
    )(*args)


def _conv_body(a_ref, g_ref, ap_ref, gp_ref, an_ref, gn_ref, w_ref, cb_ref, lg_ref, lb_ref, o_ref, u_scr, y_scr,
               *, nql, n_lat_tiles):
    t = pl.program_id(0)
    is_lat = t < n_lat_tiles
    i = t % nql
    has_prev = jnp.logical_and(is_lat, i > 0)
    has_next = jnp.logical_and(is_lat, i < nql - 1)
    up = ap_ref[...] * jax.nn.sigmoid(gp_ref[...])
    un = an_ref[...] * jax.nn.sigmoid(gn_ref[...])
    u_scr[0:CONV_HALO, :] = jnp.where(has_prev, up, 0.0)
    u_scr[CONV_HALO:CONV_HALO + TOK, :] = a_ref[...] * jax.nn.sigmoid(g_ref[...])
    u_scr[CONV_HALO + TOK:, :] = jnp.where(has_next, un, 0.0)
    base = CONV_HALO - CONV_K // 2
    for lc in range(C_CH // LANES):
        ls = slice(lc * LANES, (lc + 1) * LANES)
        acc = jnp.zeros((TOK, LANES), F32)
        for k in range(CONV_K):
            acc = acc + u_scr[base + k:base + k + TOK, ls] * w_ref[k:k + 1, ls]
        y_scr[:, ls] = acc + cb_ref[:, ls]
    y = _layer_norm_rows(y_scr[...], lg_ref[...], lb_ref[...])
    o_ref[...] = (y * jax.nn.sigmoid(y)).astype(o_ref.dtype)


def _conv(p, conv_w, conv_b, ln_g, ln_b, *, seq, n_lat_rows, n_rows):
    nql = seq // TOK
    hb = TOK // CONV_HALO
    last_hb = p.shape[0] // CONV_HALO - 1

    def cur(off):
        return pl.BlockSpec((TOK, C_CH), lambda t: (t, off // C_CH))

    def prev(off):
        return pl.BlockSpec((CONV_HALO, C_CH), lambda t: (jnp.maximum(t * hb - 1, 0), off // C_CH))

    def nxt(off):
        return pl.BlockSpec((CONV_HALO, C_CH), lambda t: (jnp.minimum((t + 1) * hb, last_hb), off // C_CH))

    def full(a):
        return pl.BlockSpec(a.shape, lambda t: (0,) * a.ndim)

    return pl.pallas_call(
        functools.partial(_conv_body, nql=nql, n_lat_tiles=n_lat_rows // TOK),
        grid=(n_rows // TOK,),
        in_specs=[cur(P_GLU_A), cur(P_GLU_G), prev(P_GLU_A), prev(P_GLU_G), nxt(P_GLU_A), nxt(P_GLU_G),
                  full(conv_w), full(conv_b), full(ln_g), full(ln_b)],
        out_specs=pl.BlockSpec((TOK, C_CH), lambda t: (t, 0)),
        out_shape=jax.ShapeDtypeStruct((n_rows, C_CH), BF16),
        scratch_shapes=[pltpu.VMEM((TOK + 2 * CONV_HALO, C_CH), F32), pltpu.VMEM((TOK, C_CH), F32)],
        compiler_params=_cparams("parallel"),
        name="conformer_conv",
    )(p, p, p, p, p, p, conv_w, conv_b, ln_g, ln_b)


def _merge_body(xa_ref, xb_ref, xc_ref, xd_ref, wa_ref, wb_ref, wc_ref, wd_ref,
                ga_ref, gb_ref, gc_ref, gd_ref, o_ref):
    acc = None
    for x_ref, w_ref, g_ref in ((xa_ref, wa_ref, ga_ref), (xb_ref, wb_ref, gb_ref),
                                (xc_ref, wc_ref, gc_ref), (xd_ref, wd_ref, gd_ref)):
        term = jax.nn.sigmoid(g_ref[...]) * jnp.dot(x_ref[...], w_ref[...], preferred_element_type=F32)
        acc = term if acc is None else acc + term
    o_ref[...] = acc.astype(o_ref.dtype)


def _merge(branches, w_brs, p, *, n_rows, d_model):
    tm, tn = MERGE_TM, MERGE_TN
    x_spec = pl.BlockSpec((tm, 1024), lambda i, j: (i, 0))
    w_spec = pl.BlockSpec((1024, tn), lambda i, j: (0, j))

    def gate_spec(n):
        return pl.BlockSpec((tm, tn), lambda i, j: (i, (P_GATE + n * d_model) // tn + j))

    return pl.pallas_call(
        _merge_body,
        grid=(n_rows // tm, d_model // tn),
        in_specs=[x_spec] * 4 + [w_spec] * 4 + [gate_spec(n) for n in range(N_BRANCH)],
        out_specs=pl.BlockSpec((tm, tn), lambda i, j: (i, j)),
        out_shape=jax.ShapeDtypeStruct((n_rows, d_model), BF16),
        compiler_params=_cparams("parallel", "arbitrary"),
        name="branch_merge",
    )(*branches, *w_brs, p, p, p, p)


def _wo_ln_body(acc_ref, w_ref, x_ref, g1_ref, lg_ref, lb_ref, o_ref, *, alpha):
    mix = jnp.dot(acc_ref[...], w_ref[...], preferred_element_type=F32)
    z = alpha * x_ref[...] + g1_ref[0] * mix
    o_ref[...] = _layer_norm_rows(z, lg_ref[...], lb_ref[...])


def _wo_ln(acc, w_o, x_all, g1, ln_g, ln_b, *, n_rows, mod_row, alpha):
    d_model = w_o.shape[0]
    row = pl.BlockSpec((TOK, d_model), lambda i: (i, 0))
    vec = pl.BlockSpec((1, d_model), lambda i: (0, 0))
    return pl.pallas_call(
        functools.partial(_wo_ln_body, alpha=alpha),
        grid=(n_rows // TOK,),
        in_specs=[row, pl.BlockSpec(w_o.shape, lambda i: (0, 0)), row,
                  pl.BlockSpec((1, 1, d_model), lambda i: (mod_row(i), 0, 0)), vec, vec],
        out_specs=row,
        out_shape=jax.ShapeDtypeStruct((n_rows, d_model), F32),
        compiler_params=_cparams("parallel"),
        name="wo_deepnorm",
    )(acc, w_o, x_all, g1, ln_g, ln_b)


def _topk_rows(s, n_out, rowi):
    n = s.shape[0]
    vals, poss = [], []
    for _ in range(n_out):
        m = jnp.max(s, axis=0, keepdims=True)
        pos = jnp.min(jnp.where(s == m, rowi, n), axis=0, keepdims=True)
        s = jnp.where(rowi == pos, -jnp.inf, s)
        vals.append(m)
        poss.append(pos)
    return vals, poss


CAND_COUNTS = tuple(PEER_TOPK // (a + 1) for a in range(PEER_TOPK))
N_CAND = sum(CAND_COUNTS)
N_CAND_PAD = -(-N_CAND // 8) * 8


def _peer_topk_body(q_ref, sk_ref, idx_ref, g_ref, ts_scr, ti_scr, cs_scr, ci_scr):
    k = PEER_TOPK
    row_keys = lax.broadcasted_iota(jnp.int32, (PEER_NKEYS, TOK), 0)
    row_cand = lax.broadcasted_iota(jnp.int32, (N_CAND_PAD, TOK), 0)
    cs_scr[N_CAND:, :] = jnp.full((N_CAND_PAD - N_CAND, TOK), -jnp.inf, F32)
    ci_scr[N_CAND:, :] = jnp.zeros((N_CAND_PAD - N_CAND, TOK), jnp.int32)

    def head(h, carry):
        for part in range(2):
            s_t = lax.dot_general(sk_ref[h, part], q_ref[2 * h + part],
                                  (((1,), (1,)), ((), ())), preferred_element_type=F32)
            vals, poss = _topk_rows(s_t, k, row_keys)
            for a in range(k):
                ts_scr[part, a:a + 1, :] = vals[a]
                ti_scr[part, a:a + 1, :] = poss[a]
        off = 0
        for a, nb in enumerate(CAND_COUNTS):
            cs_scr[off:off + nb, :] = ts_scr[0, a:a + 1, :] + ts_scr[1, 0:nb, :]
            ci_scr[off:off + nb, :] = ti_scr[0, a:a + 1, :] * PEER_NKEYS + ti_scr[1, 0:nb, :]
            off += nb
        vals, poss = _topk_rows(cs_scr[...], k, row_cand)
        cand_i = ci_scr[...]
        exps = [jnp.exp(v - vals[0]) for v in vals]
        denom = exps[0]
        for e in exps[1:]:
            denom = denom + e
        inv = 1.0 / denom
        for a in range(k):
            eid = jnp.sum(jnp.where(row_cand == poss[a], cand_i, 0), axis=0, keepdims=True)
            idx_ref[h, a:a + 1, :] = eid
            g_ref[h, a:a + 1, :] = exps[a] * inv
        return carry

    lax.fori_loop(0, PEER_HEADS, head, 0)


def _peer_topk(q, subkeys, *, n_rows):
    k = PEER_TOPK
    out_spec = pl.BlockSpec((PEER_HEADS, k, TOK), lambda t: (0, 0, t))
    idx, gsel = pl.pallas_call(
        _peer_topk_body,
        grid=(n_rows // TOK,),
        in_specs=[pl.BlockSpec((q.shape[0], TOK, LANES), lambda t: (0, t, 0)),
                  pl.BlockSpec(subkeys.shape, lambda t: (0, 0, 0, 0))],
        out_specs=[out_spec, out_spec],
        out_shape=[jax.ShapeDtypeStruct((PEER_HEADS, k, n_rows), jnp.int32),
                   jax.ShapeDtypeStruct((PEER_HEADS, k, n_rows), F32)],
        scratch_shapes=[pltpu.VMEM((2, k, TOK), F32), pltpu.VMEM((2, k, TOK), jnp.int32),
                        pltpu.VMEM((N_CAND_PAD, TOK), F32), pltpu.VMEM((N_CAND_PAD, TOK), jnp.int32)],
        compiler_params=_cparams("parallel"),
        name="peer_topk",
    )(q, subkeys)
    return idx.reshape(PEER_HEADS * k, n_rows), gsel.reshape(PEER_HEADS * k, n_rows)


HI16 = -65536


def _peer_pack_body(u_ref, v_ref, o_ref):
    ub = lax.bitcast_convert_type(u_ref[...].astype(BF16).astype(F32), jnp.int32)
    vb = lax.bitcast_convert_type(v_ref[...].astype(BF16).astype(F32), jnp.int32)
    o_ref[...] = lax.shift_right_logical(ub, 16) | (vb & HI16)


def _peer_pack(u, v):
    n_exp, d_model = u.shape
    spec = pl.BlockSpec((PACK_TM, d_model), lambda i: (i, 0))
    return pl.pallas_call(
        _peer_pack_body,
        grid=(n_exp // PACK_TM,),
        in_specs=[spec, spec],
        out_specs=spec,
        out_shape=jax.ShapeDtypeStruct((n_exp, d_model), jnp.int32),
        compiler_params=_cparams("parallel"),
        name="peer_pack",
    )(u, v)


def _peer_gather_body(idx_ref, g_ref, x_ref, s2_ref, sh2_ref, g2_ref, lg_ref, lb_ref, uv_hbm, o_ref,
                      *scratch, alpha):
    bufs = scratch[:PEER_NBUF]
    h_scr, y_scr, sem = scratch[PEER_NBUF:]
    n_sel, d_model = bufs[0].shape
    sub = 8
    h_scr[...] = x_ref[...] * (1.0 + s2_ref[0]) + sh2_ref[0]

    def issue(t, slot):
        for k in range(n_sel):
            pltpu.make_async_copy(uv_hbm.at[idx_ref[t, k]], bufs[slot].at[pl.ds(k, 1)],
                                  sem.at[slot]).start(priority=k % 2)

    def wait(slot):
        pltpu.make_async_copy(uv_hbm.at[pl.ds(0, n_sel), 0], bufs[slot], sem.at[slot]).wait()

    lane = lax.broadcasted_iota(jnp.int32, (n_sel, PEER_TT), 1)

    def token(t, slot):
        wait(slot)
        issue(jnp.minimum(t + PEER_AHEAD, PEER_TT - 1), (slot + PEER_AHEAD) % PEER_NBUF)
        buf = bufs[slot]
        h = h_scr[pl.ds(t, 1), :]
        gcol = jnp.sum(jnp.where(lane == t, g_ref[...], 0.0), axis=1, keepdims=True)
        yacc = jnp.zeros((sub, d_model), F32)
        for j in range(n_sel // sub):
            w = buf[j * sub:(j + 1) * sub, :]
            u = lax.bitcast_convert_type(lax.shift_left(w, 16), F32)
            a = jnp.sum(u * h, axis=1, keepdims=True)
            act = 0.5 * a * (1.0 + lax.erf(a * 0.7071067811865476))
            v = lax.bitcast_convert_type(w & HI16, F32)
            yacc = yacc + v * (act * gcol[j * sub:(j + 1) * sub])
        y_scr[pl.ds(t, 1), :] = jnp.sum(yacc, axis=0, keepdims=True)

    def ring(i, carry):
        for s in range(PEER_NBUF):
            token(PEER_NBUF * i + s, s)
        return carry

    for s in range(PEER_AHEAD):
        issue(s, s)
    lax.fori_loop(0, PEER_TT // PEER_NBUF, ring, 0)
    for s in range(PEER_AHEAD):
        wait(s)
    z = alpha * x_ref[...] + g2_ref[0] * y_scr[...]
    o_ref[...] = _layer_norm_rows(z, lg_ref[...], lb_ref[...])


def _peer_gather(idx_t, g_t, x1, mods, ln_g, ln_b, uv, *, n_rows, mod_row, alpha):
    d_model = x1.shape[1]
    n_sel = idx_t.shape[1]
    row = pl.BlockSpec((PEER_TT, d_model), lambda t: (t, 0))
    vec = pl.BlockSpec((1, d_model), lambda t: (0, 0))
    mod = pl.BlockSpec((1, 1, d_model), lambda t: (mod_row(t), 0, 0))
    return pl.pallas_call(
        functools.partial(_peer_gather_body, alpha=alpha),
        grid=(n_rows // PEER_TT,),
        in_specs=[pl.BlockSpec((PEER_TT, n_sel), lambda t: (t, 0), memory_space=pltpu.SMEM),
                  pl.BlockSpec((n_sel, PEER_TT), lambda t: (0, t)),
                  row, mod, mod, mod, vec, vec, pl.BlockSpec(memory_space=pl.ANY)],
        out_specs=row,
        out_shape=jax.ShapeDtypeStruct((n_rows, d_model), F32),
        scratch_shapes=[pltpu.VMEM((n_sel, d_model), jnp.int32) for _ in range(PEER_NBUF)]
        + [pltpu.VMEM((PEER_TT, d_model), F32), pltpu.VMEM((PEER_TT, d_model), F32),
           pltpu.SemaphoreType.DMA((PEER_NBUF,))],
        compiler_params=_cparams("arbitrary"),
        name="peer_gather",
    )(idx_t, g_t, x1, *mods, ln_g, ln_b, uv)


def _rope_tables(seq):
    rows = seq // GRID_W
    row = jnp.repeat(jnp.arange(rows, dtype=F32), GRID_W)
    col = jnp.tile(jnp.arange(GRID_W, dtype=F32), rows)

    def tables(dim):
        half = dim // 2
        freq = ROPE_THETA ** (-jnp.arange(0, half, 2, dtype=F32) / half)
        ang = jnp.concatenate([row[:, None] * freq, col[:, None] * freq], axis=-1)
        return jnp.cos(ang), jnp.sin(ang)

    def with_identity(cos_t, sin_t):
        return (jnp.concatenate([cos_t, jnp.ones((TOK, LANES), F32)], axis=0),
                jnp.concatenate([sin_t, jnp.zeros((TOK, LANES), F32)], axis=0))

    cos_h, sin_h = tables(HEAD_DIM)
    cos_r, sin_r = tables(B_ROPE)
    cosh, sinh = with_identity(jnp.concatenate([cos_h, cos_h], -1), jnp.concatenate([-sin_h, sin_h], -1))
    cosr, sinr = with_identity(jnp.tile(cos_r, (1, 4)), jnp.tile(jnp.concatenate([-sin_r, sin_r], -1), (1, 2)))
    return cosh, sinh, cosr, sinr


def _permute_w_in(w):
    d_model = w.shape[0]
    a_w, a_kv_w = A_HEADS * HEAD_DIM, A_KV_HEADS * HEAD_DIM
    d_w, d_kv_w = D_HEADS * HEAD_DIM, D_KV_HEADS * HEAD_DIM
    off_ak = 0
    off_av = off_ak + a_kv_w
    off_bckv = off_av + a_kv_w
    off_bkr = off_bckv + B_KV_RANK
    off_dk = off_bkr + B_ROPE
    off_dv = off_dk + d_kv_w
    off_aq = off_dv + d_kv_w
    off_bcq = off_aq + a_w
    off_dq = off_bcq + B_Q_RANK
    off_glu = off_dq + d_w
    off_gate = off_glu + 2 * C_CH

    def seg(off, n):
        return w[:, off:off + n]

    out = jnp.concatenate(
        [seg(off_aq, a_w), seg(off_dq, d_w), seg(off_glu, 2 * C_CH), seg(off_gate, N_BRANCH * d_model),
         seg(off_ak, a_kv_w), seg(off_av, a_kv_w), seg(off_bckv, B_KV_RANK), seg(off_dk, d_kv_w), seg(off_dv, d_kv_w),
         seg(off_bkr, B_ROPE), jnp.zeros((d_model, KV_W - 1280 - B_ROPE), w.dtype), seg(off_bcq, B_Q_RANK)], axis=1)
    assert out.shape[1] == NP_COLS
    return out.astype(BF16)


def kernel(x, c, ctx, c_ctx, w_ada, b_ada, w_in, a_q_norm, a_k_norm, b_q_norm, b_w_q_up, b_kv_norm, b_w_kv_up,
           c_conv_w, c_conv_b, c_ln_g, c_ln_b, d_sink, w_br_a, w_br_b, w_br_c, w_br_d, w_o,
           ln1_g, ln1_b, ln2_g, ln2_b, peer_wq, peer_subkeys, peer_u, peer_v):
    nb, seq, d_model = x.shape
    depth = w_ada.shape[0]
    assert ctx.shape[1] == TOK and seq % MM_TM == 0 and (nb * TOK) % MM_TM == 0
    assert d_model == 2048 and P_GATE + N_BRANCH * d_model == P_KV
    rl, rc = nb * seq, nb * TOK
    r_all = rl + rc
    nql = seq // TOK
    alpha = (2 * depth) ** 0.25
    scale_hd = HEAD_DIM ** -0.5
    scale_mla = B_QK ** -0.5

    def mod_row(tile):
        return lambda i: jnp.where(i < rl // tile, i // (seq // tile), nb)

    tabs = _rope_tables(seq)
    x_all = jnp.concatenate([x.reshape(rl, d_model), ctx.reshape(rc, d_model)], axis=0)
    n_mod = 16
    cvec = jnp.concatenate([c, c_ctx[None, :], jnp.zeros((n_mod - nb - 1, d_model), F32)], axis=0)

    for l in range(depth):
        last = l == depth - 1
        n_rows = rl if last else r_all
        n_q_tiles = nql if last else nql + 1

        mod = _mm(cvec, w_ada[l], n_rows=n_mod, tm=n_mod, tn=ADA_TN, out_dtype=F32, prologue="silu",
                  bias=b_ada[l][None, :], name="adaln")
        sh1, s1, g1, sh2, s2, g2 = [mod[:, i * d_model:(i + 1) * d_model].reshape(n_mod, 1, d_model)
                                    for i in range(6)]

        p = _mm(x_all, _permute_w_in(w_in[l]), n_rows=r_all, tm=MM_TM, tn=MM_TN, out_dtype=F32,
                prologue="mod", mod=(s1, sh1), mod_row=mod_row(MM_TM), name="in_proj")

        wq = b_w_q_up[l].reshape(B_Q_RANK, B_HEADS, B_QK)
        wq = jnp.concatenate([wq[:, :, :B_NOPE].reshape(B_Q_RANK, -1), wq[:, :, B_NOPE:].reshape(B_Q_RANK, -1)],
                             axis=1).astype(BF16)
        wkv = b_w_kv_up[l].reshape(B_KV_RANK, B_HEADS, B_NOPE + B_VDIM)
        wkv = jnp.concatenate([wkv[:, :, :B_NOPE].reshape(B_KV_RANK, -1), wkv[:, :, B_NOPE:].reshape(B_KV_RANK, -1)],
                              axis=1).astype(BF16)
        norms = (a_q_norm[l][None, :], a_k_norm[l][None, :], b_q_norm[l][None, :], b_kv_norm[l][None, :])
        qa, qb, qd, ka, va, kb, vb, kd, vd = _prep(p, tabs, norms, wq, wkv, nb=nb, seq=seq, n_lat_rows=rl,
                                                   n_rows=r_all)

        attn = functools.partial(_attention, nb=nb, seq=seq, n_q_tiles=n_q_tiles)
        br_a = attn(qa, ka, va, group=A_HEADS // A_KV_HEADS, dk=HEAD_DIM, scale=scale_hd, name="attn_axial")
        br_b = attn(qb, kb, vb, group=4, dk=2 * B_NOPE, scale=scale_mla, name="attn_mla")
        br_d = attn(qd, kd, vd, group=D_HEADS // D_KV_HEADS, dk=HEAD_DIM, scale=scale_hd, sink=d_sink[l],
                    name="attn_window")
        br_c = _conv(p, c_conv_w[l], c_conv_b[l][None, :], c_ln_g[l][None, :], c_ln_b[l][None, :],
                     seq=seq, n_lat_rows=rl, n_rows=n_rows)

        w_brs = [w.astype(BF16) for w in (w_br_a[l], w_br_b[l], w_br_c[l], w_br_d[l])]
        acc = _merge((br_a, br_b, br_c, br_d), w_brs, p, n_rows=n_rows, d_model=d_model)
        x1 = _wo_ln(acc, w_o[l].astype(BF16), x_all, g1, ln1_g[l][None, :], ln1_b[l][None, :],
                    n_rows=n_rows, mod_row=mod_row(TOK), alpha=alpha)

        q = _mm(x1, peer_wq[l].astype(BF16), n_rows=n_rows, tm=MM_TM, tn=MM_TN, out_dtype=BF16,
                prologue="mod", mod=(s2, sh2), mod_row=mod_row(MM_TM), split_cols=True, name="peer_query")
        idx, gsel = _peer_topk(q, peer_subkeys[l].astype(BF16), n_rows=n_rows)
        x_all = _peer_gather(idx.T, gsel, x1, (s2, sh2, g2), ln2_g[l][None, :], ln2_b[l][None, :],
                             _peer_pack(peer_u[l], peer_v[l])[:, None, :], n_rows=n_rows,
                             mod_row=mod_row(PEER_TT), alpha=alpha)

    return x_all[:rl].reshape(nb, seq, d_model)
```

```python
import functools

import jax
import jax.numpy as jnp
from jax import lax
from jax.experimental import pallas as pl
from jax.experimental.pallas import tpu as pltpu

F32 = jnp.float32
BF16 = jnp.bfloat16

GRID_W = 64
HEAD_DIM = 128
ROPE_THETA = 10000.0
A_HEADS, A_KV_HEADS = 8, 2
B_HEADS, B_Q_RANK, B_KV_RANK, B_NOPE, B_ROPE, B_VDIM = 8, 512, 256, 128, 64, 128
B_QK = B_NOPE + B_ROPE
C_CH, CONV_K = 1024, 31
D_HEADS, D_KV_HEADS = 8, 2
WINDOW = 128
N_BRANCH = 4
PEER_HEADS, PEER_NKEYS, PEER_TOPK, PEER_QDIM = 8, 128, 16, 256
RMS_EPS = 1e-6
LN_EPS = 1e-5
NEG_INF = -1e30

LANES = 128
TOK = 256
MM_TM = 1024
MM_TN = 1024
MERGE_TM = 1024
MERGE_TN = 512
WO_TM = 256
PACK_TM = 512
ADA_TN = 512
PEER_TT = 256
PEER_NBUF = 8
PEER_AHEAD = 5
CONV_HALO = 16
VMEM_LIMIT = 56 * 1024 * 1024

P_AQ, P_DQ, P_GLU_A, P_GLU_G, P_GATE = 0, 1024, 2048, 3072, 4096
P_KV = 12288
KV_W = 1536
P_BCQ = 13824
NP_COLS = 14336


def _cparams(*sem):
    return pltpu.CompilerParams(dimension_semantics=sem, vmem_limit_bytes=VMEM_LIMIT)


def _layer_norm_rows(z, g, b):
    mu = jnp.mean(z, axis=-1, keepdims=True)
    zc = z - mu
    var = jnp.mean(zc * zc, axis=-1, keepdims=True)
    return zc * lax.rsqrt(var + LN_EPS) * g + b


def _mm_body(*refs, prologue, has_bias):
    refs = list(refs)
    x_ref = refs.pop(0)
    s_ref = sh_ref = b_ref = None
    if prologue == "mod":
        s_ref, sh_ref = refs.pop(0), refs.pop(0)
    w_ref = refs.pop(0)
    if has_bias:
        b_ref = refs.pop(0)
    o_ref, h_scr = refs

    @pl.when(pl.program_id(1) == 0)
    def _():
        xv = x_ref[...].astype(F32)
        if prologue == "mod":
            xv = xv * (1.0 + s_ref[0]) + sh_ref[0]
        elif prologue == "silu":
            xv = xv * jax.nn.sigmoid(xv)
        h_scr[...] = xv.astype(BF16)

    acc = jnp.dot(h_scr[...], w_ref[...].astype(BF16), preferred_element_type=F32)
    if has_bias:
        acc = acc + b_ref[...]
    if len(o_ref.shape) == 3:
        for c in range(o_ref.shape[0]):
            o_ref[c] = acc[:, c * LANES:(c + 1) * LANES].astype(o_ref.dtype)
    else:
        o_ref[...] = acc.astype(o_ref.dtype)


def _mm(x, w, *, n_rows, tm, tn, out_dtype, prologue="none", mod=None, mod_row=None, bias=None,
        split_cols=False, w_layer=None, name=None):
    k_dim = x.shape[1]
    n_dim = w.shape[-1]
    if split_cols:
        out_spec = pl.BlockSpec((tn // LANES, tm, LANES), lambda i, j: (j, i, 0))
        out_shape = jax.ShapeDtypeStruct((n_dim // LANES, n_rows, LANES), out_dtype)
    else:
        out_spec = pl.BlockSpec((tm, tn), lambda i, j: (i, j))
        out_shape = jax.ShapeDtypeStruct((n_rows, n_dim), out_dtype)
    in_specs = [pl.BlockSpec((tm, k_dim), lambda i, j: (i, 0))]
    args = [x]
    if prologue == "mod":
        spec = pl.BlockSpec((1, 1, k_dim), lambda i, j: (mod_row(i), 0, 0))
        in_specs += [spec, spec]
        args += list(mod)
    if w_layer is None:
        in_specs.append(pl.BlockSpec((k_dim, tn), lambda i, j: (0, j)))
    else:
        in_specs.append(pl.BlockSpec((None, k_dim, tn), lambda i, j: (w_layer, 0, j)))
    args.append(w)
    if bias is not None:
        in_specs.append(pl.BlockSpec((1, tn), lambda i, j: (0, j)))
        args.append(bias)
    return pl.pallas_call(
        functools.partial(_mm_body, prologue=prologue, has_bias=bias is not None),
        grid=(n_rows // tm, n_dim // tn),
        in_specs=in_specs,
        out_specs=out_spec,
        out_shape=out_shape,
        scratch_shapes=[pltpu.VMEM((tm, k_dim), BF16)],
        compiler_params=_cparams("parallel", "arbitrary"),
        name=name,
    )(*args)


def _prep_body(aq_ref, dq_ref, kv_ref, cq_ref, cosh_ref, sinh_ref, cosr_ref, sinr_ref,
               aqn_ref, akn_ref, bqn_ref, bkvn_ref, wq_ref, wkv_ref,
               qa_ref, qb_ref, qd_ref, ka_ref, va_ref, kb_ref, vb_ref, kd_ref, vd_ref):
    cosh, sinh = cosh_ref[...], sinh_ref[...]
    cosr, sinr = cosr_ref[...], sinr_ref[...]
    lane = lax.broadcasted_iota(jnp.int32, (TOK, LANES), 1)
    low32 = (lane % 64) < 32
    low64 = lane < 64

    def rope_head(xh):
        return xh * cosh + pltpu.roll(xh, 64, 1) * sinh

    def rope_r64(xc):
        sw = jnp.where(low32, pltpu.roll(xc, 96, 1), pltpu.roll(xc, 32, 1))
        return xc * cosr + sw * sinr

    def rms(xh, g):
        ms = jnp.mean(xh * xh, axis=-1, keepdims=True)
        return xh * lax.rsqrt(ms + RMS_EPS) * g

    aqn, akn = aqn_ref[...], akn_ref[...]
    for h in range(A_HEADS):
        sl = slice(h * HEAD_DIM, (h + 1) * HEAD_DIM)
        qa_ref[:, sl] = rope_head(rms(aq_ref[:, sl], aqn)).astype(BF16)
        qd_ref[:, sl] = rope_head(dq_ref[:, sl]).astype(BF16)
    for h in range(A_KV_HEADS):
        sl = slice(h * HEAD_DIM, (h + 1) * HEAD_DIM)
        ka_ref[h] = rope_head(rms(kv_ref[:, sl], akn)).astype(BF16)
        va_ref[h] = kv_ref[:, 256 + h * HEAD_DIM:256 + (h + 1) * HEAD_DIM].T.astype(BF16)
        kd_ref[h] = rope_head(kv_ref[:, 768 + h * HEAD_DIM:768 + (h + 1) * HEAD_DIM]).astype(BF16)
        vd_ref[h] = kv_ref[:, 1024 + h * HEAD_DIM:1024 + (h + 1) * HEAD_DIM].T.astype(BF16)

    ckv = rms(kv_ref[:, 512:768], bkvn_ref[...]).astype(BF16)
    kv_up = jnp.dot(ckv, wkv_ref[...], preferred_element_type=F32)
    kr = rope_r64(kv_ref[:, 1280:1408]).astype(BF16)
    for h in range(B_HEADS):
        kb_ref[h, :, 0:B_NOPE] = kv_up[:, h * B_NOPE:(h + 1) * B_NOPE].astype(BF16)
        kb_ref[h, :, B_NOPE:2 * B_NOPE] = kr
        vb_ref[h] = kv_up[:, 1024 + h * B_VDIM:1024 + (h + 1) * B_VDIM].T.astype(BF16)

    cq = rms(cq_ref[...], bqn_ref[...]).astype(BF16)
    q_up = jnp.dot(cq, wq_ref[...], preferred_element_type=F32)
    for c in range(B_HEADS // 2):
        rc = rope_r64(q_up[:, 1024 + c * LANES:1024 + (c + 1) * LANES])
        for half in range(2):
            h = 2 * c + half
            piece = rc if half == 0 else pltpu.roll(rc, 64, 1)
            qb_ref[:, h * 256:h * 256 + B_NOPE] = q_up[:, h * B_NOPE:(h + 1) * B_NOPE].astype(BF16)
            qb_ref[:, h * 256 + B_NOPE:(h + 1) * 256] = jnp.where(low64, piece, 0.0).astype(BF16)


def _prep(p, tabs, norms, wq, wkv, *, nb, seq, n_lat_rows, n_rows):
    nql = seq // TOK
    n_lat_tiles = n_lat_rows // TOK
    t_len = seq + TOK

    def colspec(width, off):
        return pl.BlockSpec((TOK, width), lambda t: (t, off // width))

    def tabspec():
        return pl.BlockSpec((TOK, LANES), lambda t: (jnp.where(t < n_lat_tiles, t % nql, nql), 0))

    def full(a):
        return pl.BlockSpec(a.shape, lambda t: (0,) * a.ndim)

    def kvspec(heads, d):
        return pl.BlockSpec(
            (None, heads, TOK, d),
            lambda t: (jnp.where(t < n_lat_tiles, t // nql, t - n_lat_tiles), 0,
                       jnp.where(t < n_lat_tiles, t % nql, nql), 0))

    def vtspec(heads):
        return pl.BlockSpec(
            (None, heads, None, HEAD_DIM, TOK),
            lambda t: (jnp.where(t < n_lat_tiles, t // nql, t - n_lat_tiles), 0,
                       jnp.where(t < n_lat_tiles, t % nql, nql), 0, 0))

    def rowspec(width):
        return pl.BlockSpec((TOK, width), lambda t: (t, 0))

    def kvshape(heads, d):
        return jax.ShapeDtypeStruct((nb, heads, t_len, d), BF16)

    def vtshape(heads):
        return jax.ShapeDtypeStruct((nb, heads, nql + 1, HEAD_DIM, TOK), BF16)

    return pl.pallas_call(
        _prep_body,
        grid=(n_rows // TOK,),
        in_specs=[colspec(1024, P_AQ), colspec(1024, P_DQ), colspec(KV_W, P_KV), colspec(512, P_BCQ),
                  tabspec(), tabspec(), tabspec(), tabspec(),
                  full(norms[0]), full(norms[1]), full(norms[2]), full(norms[3]), full(wq), full(wkv)],
        out_specs=[rowspec(1024), rowspec(2048), rowspec(1024),
                   kvspec(2, 128), vtspec(2), kvspec(8, 256), vtspec(8), kvspec(2, 128), vtspec(2)],
        out_shape=[jax.ShapeDtypeStruct((n_rows, 1024), BF16), jax.ShapeDtypeStruct((n_rows, 2048), BF16),
                   jax.ShapeDtypeStruct((n_rows, 1024), BF16),
                   kvshape(2, 128), vtshape(2), kvshape(8, 256), vtshape(8), kvshape(2, 128), vtshape(2)],
        compiler_params=_cparams("parallel"),
        name="qkv_prep",
    )(p, p, p, p, *tabs, *norms, wq, wkv)


LOG2E = 1.4426950408889634


def _softmax_steps(qs, ks, vts, c_exp, masks, carries):
    n = len(qs)
    ss = [lax.dot_general(ks[i], qs[i], (((1,), (1,)), ((), ())), preferred_element_type=F32) for i in range(n)]
    prs, heads = [], []
    for i in range(n):
        m, l, acc = carries[i]
        s = ss[i] if masks[i] is None else jnp.where(masks[i], ss[i], NEG_INF)
        m_new = jnp.maximum(m, jnp.max(s, axis=0, keepdims=True))
        alpha = jnp.exp2((m - m_new) * c_exp)
        pr = jnp.exp2((s - m_new) * c_exp)
        prs.append(pr.astype(BF16))
        heads.append((m_new, alpha * l + jnp.sum(pr, axis=0, keepdims=True), alpha * acc))
    out = []
    for i in range(n):
        acc = heads[i][2]
        for j, vt in enumerate(vts[i]):
            acc = acc + jnp.dot(vt, prs[i][j * TOK:(j + 1) * TOK], preferred_element_type=F32)
        out.append((heads[i][0], heads[i][1], acc))
    return out


def _attn_store(o_ref, g, carry):
    _, l, acc = carry
    o_ref[:, g * HEAD_DIM:(g + 1) * HEAD_DIM] = (acc * (1.0 / l)).T.astype(o_ref.dtype)


def _attn_dense_body(q_ref, k_ref, v_ref, o_ref, *, group, dk, scale, nql, shared_kv):
    qi = pl.program_id(2)
    c_exp = scale * LOG2E
    kc = max(n for n in (4, 2, 1) if nql % n == 0)

    def step(c, carry, n_blk=kc):
        off = pl.multiple_of(c * (n_blk * TOK), TOK)
        qs = [q_ref[:, g * dk:(g + 1) * dk] for g in range(group)]
        ks = [k_ref[pl.ds(off, n_blk * TOK), :] if shared_kv else k_ref[g, pl.ds(off, n_blk * TOK), :]
              for g in range(group)]
        vts = [[v_ref[c * n_blk + j] if shared_kv else v_ref[g, c * n_blk + j] for j in range(n_blk)]
               for g in range(group)]
        return tuple(_softmax_steps(qs, ks, vts, c_exp, [None] * group, list(carry)))

    init = tuple((jnp.full((1, TOK), NEG_INF, F32), jnp.zeros((1, TOK), F32), jnp.zeros((HEAD_DIM, TOK), F32))
                 for _ in range(group))
    res = lax.fori_loop(0, jnp.where(qi >= nql, 0, nql // kc), step, init)
    res = step(nql, res, n_blk=1)
    for g in range(group):
        _attn_store(o_ref, g, res[g])


def _attn_win_body(sink_ref, q_ref, k_ref, v_ref, o_ref, *, group, scale, nql):
    kvh = pl.program_id(1)
    qi = pl.program_id(2)
    is_lat = qi < nql
    c_exp = scale * LOG2E
    kiota = lax.broadcasted_iota(jnp.int32, (TOK, TOK), 0)
    qpos = qi * TOK + lax.broadcasted_iota(jnp.int32, (TOK, TOK), 1)
    qs = [q_ref[:, g * HEAD_DIM:(g + 1) * HEAD_DIM] for g in range(group)]
    carries = [(jnp.full((1, TOK), sink_ref[kvh * group + g] * (1.0 / scale), F32), jnp.ones((1, TOK), F32),
                jnp.zeros((HEAD_DIM, TOK), F32)) for g in range(group)]
    for j in range(3):
        c = qi - 1 + j
        valid = jnp.logical_and(jnp.logical_and(c >= 0, c < nql), is_lat)
        cc = jnp.clip(c, 0, nql - 1)
        k = k_ref[pl.ds(pl.multiple_of(cc * TOK, TOK), TOK), :]
        mask = jnp.logical_and(jnp.abs(qpos - (c * TOK + kiota)) <= WINDOW, valid)
        carries = _softmax_steps(qs, [k] * group, [[v_ref[cc]]] * group, c_exp, [mask] * group, carries)
    carries = _softmax_steps(qs, [k_ref[nql * TOK:(nql + 1) * TOK, :]] * group, [[v_ref[nql]]] * group, c_exp,
                             [None] * group, carries)
    for g in range(group):
        _attn_store(o_ref, g, carries[g])


def _attention(q, k, v, *, nb, seq, n_q_tiles, group, dk, scale, sink=None, name=None):
    nql = seq // TOK
    t_len = k.shape[2]
    n_heads = q.shape[1] // dk
    shared_kv = k.shape[1] * group == n_heads
    kvh = n_heads // group

    def qrow(b, qi):
        return jnp.where(qi < nql, b * nql + qi, nb * nql + b)

    q_spec = pl.BlockSpec((TOK, group * dk), lambda b, h, qi: (qrow(b, qi), h))
    kv_heads = None if shared_kv else group
    k_spec = pl.BlockSpec((None, kv_heads, t_len, dk), lambda b, h, qi: (b, h, 0, 0))
    v_spec = pl.BlockSpec((None, kv_heads, nql + 1, HEAD_DIM, TOK), lambda b, h, qi: (b, h, 0, 0, 0))
    o_spec = pl.BlockSpec((TOK, group * HEAD_DIM), lambda b, h, qi: (qrow(b, qi), h))
    n_rows_out = nb * seq + (nb * TOK if n_q_tiles > nql else 0)
    if sink is None:
        body = functools.partial(_attn_dense_body, group=group, dk=dk, scale=scale, nql=nql, shared_kv=shared_kv)
        in_specs, args = [q_spec, k_spec, v_spec], (q, k, v)
    else:
        body = functools.partial(_attn_win_body, group=group, scale=scale, nql=nql)
        in_specs = [pl.BlockSpec(memory_space=pltpu.SMEM), q_spec, k_spec, v_spec]
        args = (sink, q, k, v)
    return pl.pallas_call(
        body,
        grid=(nb, kvh, n_q_tiles),
        in_specs=in_specs,
        out_specs=o_spec,
        out_shape=jax.ShapeDtypeStruct((n_rows_out, kvh * group * HEAD_DIM), BF16),
        compiler_params=_cparams("parallel", "parallel", "arbitrary"),
        name=name,
    )(*args)


def _conv_body(a_ref, g_ref, ap_ref, gp_ref, an_ref, gn_ref, w_ref, cb_ref, lg_ref, lb_ref, o_ref, u_scr, y_scr,
               *, nql, n_lat_tiles):
    t = pl.program_id(0)
    is_lat = t < n_lat_tiles
    i = t % nql
    has_prev = jnp.logical_and(is_lat, i > 0)
    has_next = jnp.logical_and(is_lat, i < nql - 1)
    up = ap_ref[...] * jax.nn.sigmoid(gp_ref[...])
    un = an_ref[...] * jax.nn.sigmoid(gn_ref[...])
    u_scr[0:CONV_HALO, :] = jnp.where(has_prev, up, 0.0)
    u_scr[CONV_HALO:CONV_HALO + TOK, :] = a_ref[...] * jax.nn.sigmoid(g_ref[...])
    u_scr[CONV_HALO + TOK:, :] = jnp.where(has_next, un, 0.0)
    base = CONV_HALO - CONV_K // 2
    for lc in range(C_CH // LANES):
        ls = slice(lc * LANES, (lc + 1) * LANES)
        acc = jnp.zeros((TOK, LANES), F32)
        for k in range(CONV_K):
            acc = acc + u_scr[base + k:base + k + TOK, ls] * w_ref[k:k + 1, ls]
        y_scr[:, ls] = acc + cb_ref[:, ls]
    y = _layer_norm_rows(y_scr[...], lg_ref[...], lb_ref[...])
    o_ref[...] = (y * jax.nn.sigmoid(y)).astype(o_ref.dtype)


def _conv(p, conv_w, conv_b, ln_g, ln_b, *, seq, n_lat_rows, n_rows):
    nql = seq // TOK
    hb = TOK // CONV_HALO
    last_hb = p.shape[0] // CONV_HALO - 1

    def cur(off):
        return pl.BlockSpec((TOK, C_CH), lambda t: (t, off // C_CH))

    def prev(off):
        return pl.BlockSpec((CONV_HALO, C_CH), lambda t: (jnp.maximum(t * hb - 1, 0), off // C_CH))

    def nxt(off):
        return pl.BlockSpec((CONV_HALO, C_CH), lambda t: (jnp.minimum((t + 1) * hb, last_hb), off // C_CH))

    def full(a):
        return pl.BlockSpec(a.shape, lambda t: (0,) * a.ndim)

    return pl.pallas_call(
        functools.partial(_conv_body, nql=nql, n_lat_tiles=n_lat_rows // TOK),
        grid=(n_rows // TOK,),
        in_specs=[cur(P_GLU_A), cur(P_GLU_G), prev(P_GLU_A), prev(P_GLU_G), nxt(P_GLU_A), nxt(P_GLU_G),
                  full(conv_w), full(conv_b), full(ln_g), full(ln_b)],
        out_specs=pl.BlockSpec((TOK, C_CH), lambda t: (t, 0)),
        out_shape=jax.ShapeDtypeStruct((n_rows, C_CH), BF16),
        scratch_shapes=[pltpu.VMEM((TOK + 2 * CONV_HALO, C_CH), F32), pltpu.VMEM((TOK, C_CH), F32)],
        compiler_params=_cparams("parallel"),
        name="conformer_conv",
    )(p, p, p, p, p, p, conv_w, conv_b, ln_g, ln_b)


def _merge_body(xa_ref, xb_ref, xc_ref, xd_ref, wa_ref, wb_ref, wc_ref, wd_ref,
                ga_ref, gb_ref, gc_ref, gd_ref, o_ref):
    acc = None
    for x_ref, w_ref, g_ref in ((xa_ref, wa_ref, ga_ref), (xb_ref, wb_ref, gb_ref),
                                (xc_ref, wc_ref, gc_ref), (xd_ref, wd_ref, gd_ref)):
        term = jax.nn.sigmoid(g_ref[...]) * jnp.dot(x_ref[...], w_ref[...], preferred_element_type=F32)
        acc = term if acc is None else acc + term
    o_ref[...] = acc.astype(o_ref.dtype)


def _merge(branches, w_brs, p, *, n_rows, d_model):
    tm, tn = MERGE_TM, MERGE_TN
    x_spec = pl.BlockSpec((tm, 1024), lambda i, j: (i, 0))
    w_spec = pl.BlockSpec((1024, tn), lambda i, j: (0, j))

    def gate_spec(n):
        return pl.BlockSpec((tm, tn), lambda i, j: (i, (P_GATE + n * d_model) // tn + j))

    return pl.pallas_call(
        _merge_body,
        grid=(n_rows // tm, d_model // tn),
        in_specs=[x_spec] * 4 + [w_spec] * 4 + [gate_spec(n) for n in range(N_BRANCH)],
        out_specs=pl.BlockSpec((tm, tn), lambda i, j: (i, j)),
        out_shape=jax.ShapeDtypeStruct((n_rows, d_model), BF16),
        compiler_params=_cparams("parallel", "arbitrary"),
        name="branch_merge",
    )(*branches, *w_brs, p, p, p, p)


def _wo_ln_body(acc_ref, w_ref, x_ref, g1_ref, lg_ref, lb_ref, o_ref, *, alpha):
    mix = jnp.dot(acc_ref[...], w_ref[...], preferred_element_type=F32)
    z = alpha * x_ref[...] + g1_ref[0] * mix
    o_ref[...] = _layer_norm_rows(z, lg_ref[...], lb_ref[...])


def _wo_ln(acc, w_o, x_all, g1, ln_g, ln_b, *, n_rows, mod_row, alpha):
    d_model = w_o.shape[0]
    row = pl.BlockSpec((WO_TM, d_model), lambda i: (i, 0))
    vec = pl.BlockSpec((1, d_model), lambda i: (0, 0))
    return pl.pallas_call(
        functools.partial(_wo_ln_body, alpha=alpha),
        grid=(n_rows // WO_TM,),
        in_specs=[row, pl.BlockSpec(w_o.shape, lambda i: (0, 0)), row,
                  pl.BlockSpec((1, 1, d_model), lambda i: (mod_row(i), 0, 0)), vec, vec],
        out_specs=row,
        out_shape=jax.ShapeDtypeStruct((n_rows, d_model), F32),
        compiler_params=_cparams("parallel"),
        name="wo_deepnorm",
    )(acc, w_o, x_all, g1, ln_g, ln_b)


def _topk_rows(s, n_out, rowi):
    n = s.shape[0]
    vals, poss = [], []
    for _ in range(n_out):
        m = jnp.max(s, axis=0, keepdims=True)
        pos = jnp.min(jnp.where(s == m, rowi, n), axis=0, keepdims=True)
        s = jnp.where(rowi == pos, -jnp.inf, s)
        vals.append(m)
        poss.append(pos)
    return vals, poss


CAND_COUNTS = tuple(PEER_TOPK // (a + 1) for a in range(PEER_TOPK))
N_CAND = sum(CAND_COUNTS)
N_CAND_PAD = -(-N_CAND // 8) * 8


def _peer_topk_body(q_ref, sk_ref, idx_ref, g_ref, ts_scr, ti_scr, cs_scr, ci_scr):
    k = PEER_TOPK
    row_keys = lax.broadcasted_iota(jnp.int32, (PEER_NKEYS, TOK), 0)
    row_cand = lax.broadcasted_iota(jnp.int32, (N_CAND_PAD, TOK), 0)
    cs_scr[N_CAND:, :] = jnp.full((N_CAND_PAD - N_CAND, TOK), -jnp.inf, F32)
    ci_scr[N_CAND:, :] = jnp.zeros((N_CAND_PAD - N_CAND, TOK), jnp.int32)

    def head(h, carry):
        for part in range(2):
            s_t = lax.dot_general(sk_ref[h, part], q_ref[2 * h + part],
                                  (((1,), (1,)), ((), ())), preferred_element_type=F32)
            vals, poss = _topk_rows(s_t, k, row_keys)
            for a in range(k):
                ts_scr[part, a:a + 1, :] = vals[a]
                ti_scr[part, a:a + 1, :] = poss[a]
        off = 0
        for a, nb in enumerate(CAND_COUNTS):
            cs_scr[off:off + nb, :] = ts_scr[0, a:a + 1, :] + ts_scr[1, 0:nb, :]
            ci_scr[off:off + nb, :] = ti_scr[0, a:a + 1, :] * PEER_NKEYS + ti_scr[1, 0:nb, :]
            off += nb
        vals, poss = _topk_rows(cs_scr[...], k, row_cand)
        cand_i = ci_scr[...]
        exps = [jnp.exp(v - vals[0]) for v in vals]
        denom = exps[0]
        for e in exps[1:]:
            denom = denom + e
        inv = 1.0 / denom
        for a in range(k):
            eid = jnp.sum(jnp.where(row_cand == poss[a], cand_i, 0), axis=0, keepdims=True)
            idx_ref[h, a:a + 1, :] = eid
            g_ref[h, a:a + 1, :] = exps[a] * inv
        return carry

    lax.fori_loop(0, PEER_HEADS, head, 0)


def _peer_topk(q, subkeys, *, n_rows):
    k = PEER_TOPK
    out_spec = pl.BlockSpec((PEER_HEADS, k, TOK), lambda t: (0, 0, t))
    idx, gsel = pl.pallas_call(
        _peer_topk_body,
        grid=(n_rows // TOK,),
        in_specs=[pl.BlockSpec((q.shape[0], TOK, LANES), lambda t: (0, t, 0)),
                  pl.BlockSpec(subkeys.shape, lambda t: (0, 0, 0, 0))],
        out_specs=[out_spec, out_spec],
        out_shape=[jax.ShapeDtypeStruct((PEER_HEADS, k, n_rows), jnp.int32),
                   jax.ShapeDtypeStruct((PEER_HEADS, k, n_rows), F32)],
        scratch_shapes=[pltpu.VMEM((2, k, TOK), F32), pltpu.VMEM((2, k, TOK), jnp.int32),
                        pltpu.VMEM((N_CAND_PAD, TOK), F32), pltpu.VMEM((N_CAND_PAD, TOK), jnp.int32)],
        compiler_params=_cparams("parallel"),
        name="peer_topk",
    )(q, subkeys)
    return idx.reshape(PEER_HEADS * k, n_rows), gsel.reshape(PEER_HEADS * k, n_rows)


HI16 = -65536


def _peer_pack_body(u_ref, v_ref, o_ref):
    ub = lax.bitcast_convert_type(u_ref[...].astype(BF16).astype(F32), jnp.int32)
    vb = lax.bitcast_convert_type(v_ref[...].astype(BF16).astype(F32), jnp.int32)
    o_ref[...] = lax.shift_right_logical(ub, 16) | (vb & HI16)


def _peer_pack(u, v, layer):
    _, n_exp, d_model = u.shape
    spec = pl.BlockSpec((PACK_TM, d_model), lambda i: (i, 0))
    in_spec = pl.BlockSpec((None, PACK_TM, d_model), lambda i: (layer, i, 0))
    return pl.pallas_call(
        _peer_pack_body,
        grid=(n_exp // PACK_TM,),
        in_specs=[in_spec, in_spec],
        out_specs=spec,
        out_shape=jax.ShapeDtypeStruct((n_exp, d_model), jnp.int32),
        compiler_params=_cparams("parallel"),
        name="peer_pack",
    )(u, v)


def _peer_gather_body(idx_ref, g_ref, x_ref, s2_ref, sh2_ref, g2_ref, lg_ref, lb_ref, uv_hbm, o_ref,
                      *scratch, alpha):
    bufs = scratch[:PEER_NBUF]
    h_scr, y_scr, sem = scratch[PEER_NBUF:]
    n_sel, d_model = bufs[0].shape
    sub = 8
    h_scr[...] = x_ref[...] * (1.0 + s2_ref[0]) + sh2_ref[0]

    def issue(t, slot):
        for k in range(n_sel):
            pltpu.make_async_copy(uv_hbm.at[idx_ref[t, k]], bufs[slot].at[pl.ds(k, 1)],
                                  sem.at[slot]).start(priority=k % 2)

    def wait(slot):
        pltpu.make_async_copy(uv_hbm.at[pl.ds(0, n_sel), 0], bufs[slot], sem.at[slot]).wait()

    lane = lax.broadcasted_iota(jnp.int32, (n_sel, PEER_TT), 1)

    def token(t, slot):
        wait(slot)
        issue(jnp.minimum(t + PEER_AHEAD, PEER_TT - 1), (slot + PEER_AHEAD) % PEER_NBUF)
        buf = bufs[slot]
        h = h_scr[pl.ds(t, 1), :]
        gcol = jnp.sum(jnp.where(lane == t, g_ref[...], 0.0), axis=1, keepdims=True)
        yacc = jnp.zeros((sub, d_model), F32)
        for j in range(n_sel // sub):
            w = buf[j * sub:(j + 1) * sub, :]
            u = lax.bitcast_convert_type(lax.shift_left(w, 16), F32)
            a = jnp.sum(u * h, axis=1, keepdims=True)
            act = 0.5 * a * (1.0 + lax.erf(a * 0.7071067811865476))
            v = lax.bitcast_convert_type(w & HI16, F32)
            yacc = yacc + v * (act * gcol[j * sub:(j + 1) * sub])
        y_scr[pl.ds(t, 1), :] = jnp.sum(yacc, axis=0, keepdims=True)

    def ring(i, carry):
        for s in range(PEER_NBUF):
            token(PEER_NBUF * i + s, s)
        return carry

    for s in range(PEER_AHEAD):
        issue(s, s)
    lax.fori_loop(0, PEER_TT // PEER_NBUF, ring, 0)
    for s in range(PEER_AHEAD):
        wait(s)
    z = alpha * x_ref[...] + g2_ref[0] * y_scr[...]
    o_ref[...] = _layer_norm_rows(z, lg_ref[...], lb_ref[...])


def _peer_gather(idx_t, g_t, x1, mods, ln_g, ln_b, uv, *, n_rows, mod_row, alpha):
    d_model = x1.shape[1]
    n_sel = idx_t.shape[1]
    row = pl.BlockSpec((PEER_TT, d_model), lambda t: (t, 0))
    vec = pl.BlockSpec((1, d_model), lambda t: (0, 0))
    mod = pl.BlockSpec((1, 1, d_model), lambda t: (mod_row(t), 0, 0))
    return pl.pallas_call(
        functools.partial(_peer_gather_body, alpha=alpha),
        grid=(n_rows // PEER_TT,),
        in_specs=[pl.BlockSpec((PEER_TT, n_sel), lambda t: (t, 0), memory_space=pltpu.SMEM),
                  pl.BlockSpec((n_sel, PEER_TT), lambda t: (0, t)),
                  row, mod, mod, mod, vec, vec, pl.BlockSpec(memory_space=pl.ANY)],
        out_specs=row,
        out_shape=jax.ShapeDtypeStruct((n_rows, d_model), F32),
        scratch_shapes=[pltpu.VMEM((n_sel, d_model), jnp.int32) for _ in range(PEER_NBUF)]
        + [pltpu.VMEM((PEER_TT, d_model), F32), pltpu.VMEM((PEER_TT, d_model), F32),
           pltpu.SemaphoreType.DMA((PEER_NBUF,))],
        compiler_params=_cparams("arbitrary"),
        name="peer_gather",
    )(idx_t, g_t, x1, *mods, ln_g, ln_b, uv)


def _rope_tables(seq):
    rows = seq // GRID_W
    row = jnp.repeat(jnp.arange(rows, dtype=F32), GRID_W)
    col = jnp.tile(jnp.arange(GRID_W, dtype=F32), rows)

    def tables(dim):
        half = dim // 2
        freq = ROPE_THETA ** (-jnp.arange(0, half, 2, dtype=F32) / half)
        ang = jnp.concatenate([row[:, None] * freq, col[:, None] * freq], axis=-1)
        return jnp.cos(ang), jnp.sin(ang)

    def with_identity(cos_t, sin_t):
        return (jnp.concatenate([cos_t, jnp.ones((TOK, LANES), F32)], axis=0),
                jnp.concatenate([sin_t, jnp.zeros((TOK, LANES), F32)], axis=0))

    cos_h, sin_h = tables(HEAD_DIM)
    cos_r, sin_r = tables(B_ROPE)
    cosh, sinh = with_identity(jnp.concatenate([cos_h, cos_h], -1), jnp.concatenate([-sin_h, sin_h], -1))
    cosr, sinr = with_identity(jnp.tile(cos_r, (1, 4)), jnp.tile(jnp.concatenate([-sin_r, sin_r], -1), (1, 2)))
    return cosh, sinh, cosr, sinr


def _permute_w_in(w):
    d_model = w.shape[0]
    a_w, a_kv_w = A_HEADS * HEAD_DIM, A_KV_HEADS * HEAD_DIM
    d_w, d_kv_w = D_HEADS * HEAD_DIM, D_KV_HEADS * HEAD_DIM
    off_ak = 0
    off_av = off_ak + a_kv_w
    off_bckv = off_av + a_kv_w
    off_bkr = off_bckv + B_KV_RANK
    off_dk = off_bkr + B_ROPE
    off_dv = off_dk + d_kv_w
    off_aq = off_dv + d_kv_w
    off_bcq = off_aq + a_w
    off_dq = off_bcq + B_Q_RANK
    off_glu = off_dq + d_w
    off_gate = off_glu + 2 * C_CH

    def seg(off, n):
        return w[:, off:off + n]

    out = jnp.concatenate(
        [seg(off_aq, a_w), seg(off_dq, d_w), seg(off_glu, 2 * C_CH), seg(off_gate, N_BRANCH * d_model),
         seg(off_ak, a_kv_w), seg(off_av, a_kv_w), seg(off_bckv, B_KV_RANK), seg(off_dk, d_kv_w), seg(off_dv, d_kv_w),
         seg(off_bkr, B_ROPE), jnp.zeros((d_model, KV_W - 1280 - B_ROPE), w.dtype), seg(off_bcq, B_Q_RANK)], axis=1)
    assert out.shape[1] == NP_COLS
    return out.astype(BF16)


def kernel(x, c, ctx, c_ctx, w_ada, b_ada, w_in, a_q_norm, a_k_norm, b_q_norm, b_w_q_up, b_kv_norm, b_w_kv_up,
           c_conv_w, c_conv_b, c_ln_g, c_ln_b, d_sink, w_br_a, w_br_b, w_br_c, w_br_d, w_o,
           ln1_g, ln1_b, ln2_g, ln2_b, peer_wq, peer_subkeys, peer_u, peer_v):
    nb, seq, d_model = x.shape
    depth = w_ada.shape[0]
    assert ctx.shape[1] == TOK and seq % MM_TM == 0 and (nb * TOK) % MM_TM == 0
    assert d_model == 2048 and P_GATE + N_BRANCH * d_model == P_KV
    rl, rc = nb * seq, nb * TOK
    r_all = rl + rc
    nql = seq // TOK
    alpha = (2 * depth) ** 0.25
    scale_hd = HEAD_DIM ** -0.5
    scale_mla = B_QK ** -0.5

    def mod_row(tile):
        return lambda i: jnp.where(i < rl // tile, i // (seq // tile), nb)

    tabs = _rope_tables(seq)
    x_all = jnp.concatenate([x.reshape(rl, d_model), ctx.reshape(rc, d_model)], axis=0)
    n_mod = 16
    cvec = jnp.concatenate([c, c_ctx[None, :], jnp.zeros((n_mod - nb - 1, d_model), F32)], axis=0)

    for l in range(depth):
        last = l == depth - 1
        n_rows = rl if last else r_all
        n_q_tiles = nql if last else nql + 1

        mod = _mm(cvec, w_ada, w_layer=l, n_rows=n_mod, tm=n_mod, tn=ADA_TN, out_dtype=F32, prologue="silu",
                  bias=b_ada[l][None, :], name="adaln")
        sh1, s1, g1, sh2, s2, g2 = [mod[:, i * d_model:(i + 1) * d_model].reshape(n_mod, 1, d_model)
                                    for i in range(6)]

        p = _mm(x_all, _permute_w_in(w_in[l]), n_rows=r_all, tm=MM_TM, tn=MM_TN, out_dtype=F32,
                prologue="mod", mod=(s1, sh1), mod_row=mod_row(MM_TM), name="in_proj")

        wq = b_w_q_up[l].reshape(B_Q_RANK, B_HEADS, B_QK)
        wq = jnp.concatenate([wq[:, :, :B_NOPE].reshape(B_Q_RANK, -1), wq[:, :, B_NOPE:].reshape(B_Q_RANK, -1)],
                             axis=1).astype(BF16)
        wkv = b_w_kv_up[l].reshape(B_KV_RANK, B_HEADS, B_NOPE + B_VDIM)
        wkv = jnp.concatenate([wkv[:, :, :B_NOPE].reshape(B_KV_RANK, -1), wkv[:, :, B_NOPE:].reshape(B_KV_RANK, -1)],
                              axis=1).astype(BF16)
        norms = (a_q_norm[l][None, :], a_k_norm[l][None, :], b_q_norm[l][None, :], b_kv_norm[l][None, :])
        qa, qb, qd, ka, va, kb, vb, kd, vd = _prep(p, tabs, norms, wq, wkv, nb=nb, seq=seq, n_lat_rows=rl,
                                                   n_rows=r_all)

        attn = functools.partial(_attention, nb=nb, seq=seq, n_q_tiles=n_q_tiles)
        br_a = attn(qa, ka, va, group=A_HEADS // A_KV_HEADS, dk=HEAD_DIM, scale=scale_hd, name="attn_axial")
        br_b = attn(qb, kb, vb, group=4, dk=2 * B_NOPE, scale=scale_mla, name="attn_mla")
        br_d = attn(qd, kd, vd, group=D_HEADS // D_KV_HEADS, dk=HEAD_DIM, scale=scale_hd, sink=d_sink[l],
                    name="attn_window")
        br_c = _conv(p, c_conv_w[l], c_conv_b[l][None, :], c_ln_g[l][None, :], c_ln_b[l][None, :],
                     seq=seq, n_lat_rows=rl, n_rows=n_rows)

        w_brs = [w.astype(BF16) for w in (w_br_a[l], w_br_b[l], w_br_c[l], w_br_d[l])]
        acc = _merge((br_a, br_b, br_c, br_d), w_brs, p, n_rows=n_rows, d_model=d_model)
        x1 = _wo_ln(acc, w_o[l].astype(BF16), x_all, g1, ln1_g[l][None, :], ln1_b[l][None, :],
                    n_rows=n_rows, mod_row=mod_row(WO_TM), alpha=alpha)

        q = _mm(x1, peer_wq[l].astype(BF16), n_rows=n_rows, tm=MM_TM, tn=MM_TN, out_dtype=BF16,
                prologue="mod", mod=(s2, sh2), mod_row=mod_row(MM_TM), split_cols=True, name="peer_query")
        idx, gsel = _peer_topk(q, peer_subkeys[l].astype(BF16), n_rows=n_rows)
        x_all = _peer_gather(idx.T, gsel, x1, (s2, sh2, g2), ln2_g[l][None, :], ln2_b[l][None, :],
                             _peer_pack(peer_u, peer_v, l)[:, None, :], n_rows=n_rows,
                             mod_row=mod_row(PEER_TT), alpha=alpha)

    return x_all[:rl].reshape(nb, seq, d_model)
```

```python
import functools

import jax
import jax.numpy as jnp
from jax import lax
from jax.experimental import pallas as pl
from jax.experimental.pallas import tpu as pltpu

F32 = jnp.float32
BF16 = jnp.bfloat16

GRID_W = 64
HEAD_DIM = 128
ROPE_THETA = 10000.0
A_HEADS, A_KV_HEADS = 8, 2
B_HEADS, B_Q_RANK, B_KV_RANK, B_NOPE, B_ROPE, B_VDIM = 8, 512, 256, 128, 64, 128
B_QK = B_NOPE + B_ROPE
C_CH, CONV_K = 1024, 31
D_HEADS, D_KV_HEADS = 8, 2
WINDOW = 128
N_BRANCH = 4
PEER_HEADS, PEER_NKEYS, PEER_TOPK, PEER_QDIM = 8, 128, 16, 256
RMS_EPS = 1e-6
LN_EPS = 1e-5
NEG_INF = -1e30

LANES = 128
SUBLANES = 8
TOK = 256
MM_TM = 1024
MM_TN = 1024
MERGE_TM = 1024
MERGE_TN = 512
WO_TM = 256
PACK_TM = 256
ADA_TN = 512
PEER_TT = 256
PEER_NBUF = 8
PEER_AHEAD = 5
CONV_HALO = 16
VMEM_LIMIT = 56 * 1024 * 1024

P_AQ, P_DQ, P_GLU_A, P_GLU_G, P_GATE = 0, 1024, 2048, 3072, 4096
P_KV = 12288
KV_W = 1536
P_BCQ = 13824
NP_COLS = 14336


def _cparams(*sem):
    return pltpu.CompilerParams(dimension_semantics=sem, vmem_limit_bytes=VMEM_LIMIT)


def _layer_norm_rows(z, g, b):
    mu = jnp.mean(z, axis=-1, keepdims=True)
    zc = z - mu
    var = jnp.mean(zc * zc, axis=-1, keepdims=True)
    return zc * lax.rsqrt(var + LN_EPS) * g + b


def _mm_body(*refs, prologue, has_bias):
    refs = list(refs)
    x_ref = refs.pop(0)
    s_ref = sh_ref = b_ref = None
    if prologue == "mod":
        s_ref, sh_ref = refs.pop(0), refs.pop(0)
    w_ref = refs.pop(0)
    if has_bias:
        b_ref = refs.pop(0)
    o_ref, h_scr = refs

    @pl.when(pl.program_id(1) == 0)
    def _():
        xv = x_ref[...].astype(F32)
        if prologue == "mod":
            xv = xv * (1.0 + s_ref[0]) + sh_ref[0]
        elif prologue == "silu":
            xv = xv * jax.nn.sigmoid(xv)
        h_scr[...] = xv.astype(BF16)

    acc = jnp.dot(h_scr[...], w_ref[...].astype(BF16), preferred_element_type=F32)
    if has_bias:
        acc = acc + b_ref[...]
    if len(o_ref.shape) == 3:
        for c in range(o_ref.shape[0]):
            o_ref[c] = acc[:, c * LANES:(c + 1) * LANES].astype(o_ref.dtype)
    else:
        o_ref[...] = acc.astype(o_ref.dtype)


def _mm(x, w, *, n_rows, tm, tn, out_dtype, prologue="none", mod=None, mod_row=None, bias=None,
        split_cols=False, w_layer=None, name=None):
    k_dim = x.shape[1]
    n_dim = w.shape[-1]
    if split_cols:
        out_spec = pl.BlockSpec((tn // LANES, tm, LANES), lambda i, j: (j, i, 0))
        out_shape = jax.ShapeDtypeStruct((n_dim // LANES, n_rows, LANES), out_dtype)
    else:
        out_spec = pl.BlockSpec((tm, tn), lambda i, j: (i, j))
        out_shape = jax.ShapeDtypeStruct((n_rows, n_dim), out_dtype)
    in_specs = [pl.BlockSpec((tm, k_dim), lambda i, j: (i, 0))]
    args = [x]
    if prologue == "mod":
        spec = pl.BlockSpec((1, 1, k_dim), lambda i, j: (mod_row(i), 0, 0))
        in_specs += [spec, spec]
        args += list(mod)
    if w_layer is None:
        in_specs.append(pl.BlockSpec((k_dim, tn), lambda i, j: (0, j)))
    else:
        in_specs.append(pl.BlockSpec((None, k_dim, tn), lambda i, j: (w_layer, 0, j)))
    args.append(w)
    if bias is not None:
        in_specs.append(pl.BlockSpec((1, tn), lambda i, j: (0, j)))
        args.append(bias)
    return pl.pallas_call(
        functools.partial(_mm_body, prologue=prologue, has_bias=bias is not None),
        grid=(n_rows // tm, n_dim // tn),
        in_specs=in_specs,
        out_specs=out_spec,
        out_shape=out_shape,
        scratch_shapes=[pltpu.VMEM((tm, k_dim), BF16)],
        compiler_params=_cparams("parallel", "arbitrary"),
        name=name,
    )(*args)


def _prep_body(aq_ref, dq_ref, kv_ref, cq_ref, cosh_ref, sinh_ref, cosr_ref, sinr_ref,
               aqn_ref, akn_ref, bqn_ref, bkvn_ref, wq_ref, wkv_ref,
               qa_ref, qb_ref, qd_ref, ka_ref, va_ref, kb_ref, vb_ref, kd_ref, vd_ref):
    cosh, sinh = cosh_ref[...], sinh_ref[...]
    cosr, sinr = cosr_ref[...], sinr_ref[...]
    lane = lax.broadcasted_iota(jnp.int32, (TOK, LANES), 1)
    low32 = (lane % 64) < 32
    low64 = lane < 64

    def rope_head(xh):
        return xh * cosh + pltpu.roll(xh, 64, 1) * sinh

    def rope_r64(xc):
        sw = jnp.where(low32, pltpu.roll(xc, 96, 1), pltpu.roll(xc, 32, 1))
        return xc * cosr + sw * sinr

    def rms(xh, g):
        ms = jnp.mean(xh * xh, axis=-1, keepdims=True)
        return xh * lax.rsqrt(ms + RMS_EPS) * g

    aqn, akn = aqn_ref[...], akn_ref[...]
    for h in range(A_HEADS):
        sl = slice(h * HEAD_DIM, (h + 1) * HEAD_DIM)
        qa_ref[:, sl] = rope_head(rms(aq_ref[:, sl], aqn)).astype(BF16)
        qd_ref[:, sl] = rope_head(dq_ref[:, sl]).astype(BF16)
    for h in range(A_KV_HEADS):
        sl = slice(h * HEAD_DIM, (h + 1) * HEAD_DIM)
        ka_ref[h] = rope_head(rms(kv_ref[:, sl], akn)).astype(BF16)
        va_ref[h] = kv_ref[:, 256 + h * HEAD_DIM:256 + (h + 1) * HEAD_DIM].T.astype(BF16)
        kd_ref[h] = rope_head(kv_ref[:, 768 + h * HEAD_DIM:768 + (h + 1) * HEAD_DIM]).astype(BF16)
        vd_ref[h] = kv_ref[:, 1024 + h * HEAD_DIM:1024 + (h + 1) * HEAD_DIM].T.astype(BF16)

    ckv = rms(kv_ref[:, 512:768], bkvn_ref[...]).astype(BF16)
    kv_up = jnp.dot(ckv, wkv_ref[...], preferred_element_type=F32)
    kr = rope_r64(kv_ref[:, 1280:1408]).astype(BF16)
    for h in range(B_HEADS):
        kb_ref[h, :, 0:B_NOPE] = kv_up[:, h * B_NOPE:(h + 1) * B_NOPE].astype(BF16)
        kb_ref[h, :, B_NOPE:2 * B_NOPE] = kr
        vb_ref[h] = kv_up[:, 1024 + h * B_VDIM:1024 + (h + 1) * B_VDIM].T.astype(BF16)

    cq = rms(cq_ref[...], bqn_ref[...]).astype(BF16)
    q_up = jnp.dot(cq, wq_ref[...], preferred_element_type=F32)
    for c in range(B_HEADS // 2):
        rc = rope_r64(q_up[:, 1024 + c * LANES:1024 + (c + 1) * LANES])
        for half in range(2):
            h = 2 * c + half
            piece = rc if half == 0 else pltpu.roll(rc, 64, 1)
            qb_ref[:, h * 256:h * 256 + B_NOPE] = q_up[:, h * B_NOPE:(h + 1) * B_NOPE].astype(BF16)
            qb_ref[:, h * 256 + B_NOPE:(h + 1) * 256] = jnp.where(low64, piece, 0.0).astype(BF16)


def _prep(p, tabs, norms, wq, wkv, *, nb, seq, n_lat_rows, n_rows):
    nql = seq // TOK
    n_lat_tiles = n_lat_rows // TOK
    t_len = seq + TOK

    def colspec(width, off):
        return pl.BlockSpec((TOK, width), lambda t: (t, off // width))

    def tabspec():
        return pl.BlockSpec((TOK, LANES), lambda t: (jnp.where(t < n_lat_tiles, t % nql, nql), 0))

    def full(a):
        return pl.BlockSpec(a.shape, lambda t: (0,) * a.ndim)

    def kvspec(heads, d):
        return pl.BlockSpec(
            (None, heads, TOK, d),
            lambda t: (jnp.where(t < n_lat_tiles, t // nql, t - n_lat_tiles), 0,
                       jnp.where(t < n_lat_tiles, t % nql, nql), 0))

    def vtspec(heads):
        return pl.BlockSpec(
            (None, heads, None, HEAD_DIM, TOK),
            lambda t: (jnp.where(t < n_lat_tiles, t // nql, t - n_lat_tiles), 0,
                       jnp.where(t < n_lat_tiles, t % nql, nql), 0, 0))

    def rowspec(width):
        return pl.BlockSpec((TOK, width), lambda t: (t, 0))

    def kvshape(heads, d):
        return jax.ShapeDtypeStruct((nb, heads, t_len, d), BF16)

    def vtshape(heads):
        return jax.ShapeDtypeStruct((nb, heads, nql + 1, HEAD_DIM, TOK), BF16)

    return pl.pallas_call(
        _prep_body,
        grid=(n_rows // TOK,),
        in_specs=[colspec(1024, P_AQ), colspec(1024, P_DQ), colspec(KV_W, P_KV), colspec(512, P_BCQ),
                  tabspec(), tabspec(), tabspec(), tabspec(),
                  full(norms[0]), full(norms[1]), full(norms[2]), full(norms[3]), full(wq), full(wkv)],
        out_specs=[rowspec(1024), rowspec(2048), rowspec(1024),
                   kvspec(2, 128), vtspec(2), kvspec(8, 256), vtspec(8), kvspec(2, 128), vtspec(2)],
        out_shape=[jax.ShapeDtypeStruct((n_rows, 1024), BF16), jax.ShapeDtypeStruct((n_rows, 2048), BF16),
                   jax.ShapeDtypeStruct((n_rows, 1024), BF16),
                   kvshape(2, 128), vtshape(2), kvshape(8, 256), vtshape(8), kvshape(2, 128), vtshape(2)],
        compiler_params=_cparams("parallel"),
        name="qkv_prep",
    )(p, p, p, p, *tabs, *norms, wq, wkv)


LOG2E = 1.4426950408889634


def _softmax_steps(qs, ks, vts, c_exp, masks, carries):
    n = len(qs)
    ss = [lax.dot_general(ks[i], qs[i], (((1,), (1,)), ((), ())), preferred_element_type=F32) for i in range(n)]
    prs, heads = [], []
    for i in range(n):
        m, l, acc = carries[i]
        s = ss[i] if masks[i] is None else jnp.where(masks[i], ss[i], NEG_INF)
        m_new = jnp.maximum(m, jnp.max(s, axis=0, keepdims=True))
        alpha = jnp.exp2((m - m_new) * c_exp)
        pr = jnp.exp2((s - m_new) * c_exp)
        prs.append(pr.astype(BF16))
        heads.append((m_new, alpha * l + jnp.sum(pr, axis=0, keepdims=True), alpha * acc))
    out = []
    for i in range(n):
        acc = heads[i][2]
        for j, vt in enumerate(vts[i]):
            acc = acc + jnp.dot(vt, prs[i][j * TOK:(j + 1) * TOK], preferred_element_type=F32)
        out.append((heads[i][0], heads[i][1], acc))
    return out


def _attn_store(o_ref, g, carry):
    _, l, acc = carry
    o_ref[:, g * HEAD_DIM:(g + 1) * HEAD_DIM] = (acc * (1.0 / l)).T.astype(o_ref.dtype)


def _attn_dense_body(q_ref, k_ref, v_ref, o_ref, *, group, dk, scale, nql, shared_kv):
    qi = pl.program_id(2)
    c_exp = scale * LOG2E
    kc = max(n for n in (16, 8, 4, 2, 1) if nql % n == 0)

    def step(c, carry, n_blk=kc):
        off = pl.multiple_of(c * (n_blk * TOK), TOK)
        qs = [q_ref[:, g * dk:(g + 1) * dk] for g in range(group)]
        ks = [k_ref[pl.ds(off, n_blk * TOK), :] if shared_kv else k_ref[g, pl.ds(off, n_blk * TOK), :]
              for g in range(group)]
        vts = [[v_ref[c * n_blk + j] if shared_kv else v_ref[g, c * n_blk + j] for j in range(n_blk)]
               for g in range(group)]
        return tuple(_softmax_steps(qs, ks, vts, c_exp, [None] * group, list(carry)))

    init = tuple((jnp.full((1, TOK), NEG_INF, F32), jnp.zeros((1, TOK), F32), jnp.zeros((HEAD_DIM, TOK), F32))
                 for _ in range(group))
    res = lax.fori_loop(0, jnp.where(qi >= nql, 0, nql // kc), step, init)
    res = step(nql, res, n_blk=1)
    for g in range(group):
        _attn_store(o_ref, g, res[g])


def _attn_win_body(sink_ref, q_ref, k_ref, v_ref, o_ref, *, group, scale, nql):
    kvh = pl.program_id(1)
    qi = pl.program_id(2)
    is_lat = qi < nql
    c_exp = scale * LOG2E
    n_blk = 3
    kiota = lax.broadcasted_iota(jnp.int32, (n_blk * TOK, TOK), 0)
    qpos = qi * TOK + lax.broadcasted_iota(jnp.int32, (n_blk * TOK, TOK), 1)
    qs = [q_ref[:, g * HEAD_DIM:(g + 1) * HEAD_DIM] for g in range(group)]
    carries = [(jnp.full((1, TOK), sink_ref[kvh * group + g] * (1.0 / scale), F32), jnp.ones((1, TOK), F32),
                jnp.zeros((HEAD_DIM, TOK), F32)) for g in range(group)]
    c0 = jnp.clip(qi - 1, 0, nql - n_blk)
    k = k_ref[pl.ds(pl.multiple_of(c0 * TOK, TOK), n_blk * TOK), :]
    mask = jnp.logical_and(jnp.abs(qpos - (c0 * TOK + kiota)) <= WINDOW, is_lat)
    carries = _softmax_steps(qs, [k] * group, [[v_ref[c0 + j] for j in range(n_blk)]] * group, c_exp,
                             [mask] * group, carries)
    carries = _softmax_steps(qs, [k_ref[nql * TOK:(nql + 1) * TOK, :]] * group, [[v_ref[nql]]] * group, c_exp,
                             [None] * group, carries)
    for g in range(group):
        _attn_store(o_ref, g, carries[g])


def _attention(q, k, v, *, nb, seq, n_q_tiles, group, dk, scale, sink=None, name=None):
    nql = seq // TOK
    t_len = k.shape[2]
    n_heads = q.shape[1] // dk
    shared_kv = k.shape[1] * group == n_heads
    kvh = n_heads // group

    def qrow(b, qi):
        return jnp.where(qi < nql, b * nql + qi, nb * nql + b)

    q_spec = pl.BlockSpec((TOK, group * dk), lambda b, h, qi: (qrow(b, qi), h))
    kv_heads = None if shared_kv else group
    k_spec = pl.BlockSpec((None, kv_heads, t_len, dk), lambda b, h, qi: (b, h, 0, 0))
    v_spec = pl.BlockSpec((None, kv_heads, nql + 1, HEAD_DIM, TOK), lambda b, h, qi: (b, h, 0, 0, 0))
    o_spec = pl.BlockSpec((TOK, group * HEAD_DIM), lambda b, h, qi: (qrow(b, qi), h))
    n_rows_out = nb * seq + (nb * TOK if n_q_tiles > nql else 0)
    if sink is None:
        body = functools.partial(_attn_dense_body, group=group, dk=dk, scale=scale, nql=nql, shared_kv=shared_kv)
        in_specs, args = [q_spec, k_spec, v_spec], (q, k, v)
    else:
        body = functools.partial(_attn_win_body, group=group, scale=scale, nql=nql)
        in_specs = [pl.BlockSpec(memory_space=pltpu.SMEM), q_spec, k_spec, v_spec]
        args = (sink, q, k, v)
    return pl.pallas_call(
        body,
        grid=(nb, kvh, n_q_tiles),
        in_specs=in_specs,
        out_specs=o_spec,
        out_shape=jax.ShapeDtypeStruct((n_rows_out, kvh * group * HEAD_DIM), BF16),
        compiler_params=_cparams("parallel", "parallel", "arbitrary"),
        name=name,
    )(*args)


def _conv_body(a_ref, g_ref, ap_ref, gp_ref, an_ref, gn_ref, w_ref, cb_ref, lg_ref, lb_ref, o_ref, u_scr, sh_scr,
               y_scr, *, nql, n_lat_tiles):
    t = pl.program_id(0)
    is_lat = t < n_lat_tiles
    i = t % nql
    has_prev = jnp.logical_and(is_lat, i > 0)
    has_next = jnp.logical_and(is_lat, i < nql - 1)
    up = ap_ref[...] * jax.nn.sigmoid(gp_ref[...])
    un = an_ref[...] * jax.nn.sigmoid(gn_ref[...])
    u_scr[0:CONV_HALO, :] = jnp.where(has_prev, up, 0.0)
    u_scr[CONV_HALO:CONV_HALO + TOK, :] = a_ref[...] * jax.nn.sigmoid(g_ref[...])
    u_scr[CONV_HALO + TOK:, :] = jnp.where(has_next, un, 0.0)
    base = CONV_HALO - CONV_K // 2
    n_sh = sh_scr.shape[1]
    for r in range(1, SUBLANES):
        sh_scr[r - 1] = u_scr[r:r + n_sh, :]
    for lc in range(C_CH // LANES):
        ls = slice(lc * LANES, (lc + 1) * LANES)
        acc = jnp.zeros((TOK, LANES), F32)
        for k in range(CONV_K):
            q, r = divmod(base + k, SUBLANES)
            rows = slice(q * SUBLANES, q * SUBLANES + TOK)
            tap = u_scr[rows, ls] if r == 0 else sh_scr[r - 1, rows, ls]
            acc = acc + tap * w_ref[k:k + 1, ls]
        y_scr[:, ls] = acc + cb_ref[:, ls]
    y = _layer_norm_rows(y_scr[...], lg_ref[...], lb_ref[...])
    o_ref[...] = (y * jax.nn.sigmoid(y)).astype(o_ref.dtype)


def _conv(p, conv_w, conv_b, ln_g, ln_b, *, seq, n_lat_rows, n_rows):
    nql = seq // TOK
    hb = TOK // CONV_HALO
    last_hb = p.shape[0] // CONV_HALO - 1

    def cur(off):
        return pl.BlockSpec((TOK, C_CH), lambda t: (t, off // C_CH))

    def prev(off):
        return pl.BlockSpec((CONV_HALO, C_CH), lambda t: (jnp.maximum(t * hb - 1, 0), off // C_CH))

    def nxt(off):
        return pl.BlockSpec((CONV_HALO, C_CH), lambda t: (jnp.minimum((t + 1) * hb, last_hb), off // C_CH))

    def full(a):
        return pl.BlockSpec(a.shape, lambda t: (0,) * a.ndim)

    return pl.pallas_call(
        functools.partial(_conv_body, nql=nql, n_lat_tiles=n_lat_rows // TOK),
        grid=(n_rows // TOK,),
        in_specs=[cur(P_GLU_A), cur(P_GLU_G), prev(P_GLU_A), prev(P_GLU_G), nxt(P_GLU_A), nxt(P_GLU_G),
                  full(conv_w), full(conv_b), full(ln_g), full(ln_b)],
        out_specs=pl.BlockSpec((TOK, C_CH), lambda t: (t, 0)),
        out_shape=jax.ShapeDtypeStruct((n_rows, C_CH), BF16),
        scratch_shapes=[pltpu.VMEM((TOK + 2 * CONV_HALO, C_CH), F32),
                        pltpu.VMEM((SUBLANES - 1, TOK + 2 * CONV_HALO - SUBLANES, C_CH), F32),
                        pltpu.VMEM((TOK, C_CH), F32)],
        compiler_params=_cparams("parallel"),
        name="conformer_conv",
    )(p, p, p, p, p, p, conv_w, conv_b, ln_g, ln_b)


def _merge_body(xa_ref, xb_ref, xc_ref, xd_ref, wa_ref, wb_ref, wc_ref, wd_ref,
                ga_ref, gb_ref, gc_ref, gd_ref, o_ref):
    acc = None
    for x_ref, w_ref, g_ref in ((xa_ref, wa_ref, ga_ref), (xb_ref, wb_ref, gb_ref),
                                (xc_ref, wc_ref, gc_ref), (xd_ref, wd_ref, gd_ref)):
        term = jax.nn.sigmoid(g_ref[...]) * jnp.dot(x_ref[...], w_ref[...], preferred_element_type=F32)
        acc = term if acc is None else acc + term
    o_ref[...] = acc.astype(o_ref.dtype)


def _merge(branches, w_brs, p, *, n_rows, d_model):
    tm, tn = MERGE_TM, MERGE_TN
    x_spec = pl.BlockSpec((tm, 1024), lambda i, j: (i, 0))
    w_spec = pl.BlockSpec((1024, tn), lambda i, j: (0, j))

    def gate_spec(n):
        return pl.BlockSpec((tm, tn), lambda i, j: (i, (P_GATE + n * d_model) // tn + j))

    return pl.pallas_call(
        _merge_body,
        grid=(n_rows // tm, d_model // tn),
        in_specs=[x_spec] * 4 + [w_spec] * 4 + [gate_spec(n) for n in range(N_BRANCH)],
        out_specs=pl.BlockSpec((tm, tn), lambda i, j: (i, j)),
        out_shape=jax.ShapeDtypeStruct((n_rows, d_model), BF16),
        compiler_params=_cparams("parallel", "arbitrary"),
        name="branch_merge",
    )(*branches, *w_brs, p, p, p, p)


def _wo_ln_body(acc_ref, w_ref, x_ref, g1_ref, lg_ref, lb_ref, o_ref, *, alpha):
    mix = jnp.dot(acc_ref[...], w_ref[...], preferred_element_type=F32)
    z = alpha * x_ref[...] + g1_ref[0] * mix
    o_ref[...] = _layer_norm_rows(z, lg_ref[...], lb_ref[...])


def _wo_ln(acc, w_o, x_all, g1, ln_g, ln_b, *, n_rows, mod_row, alpha):
    d_model = w_o.shape[0]
    row = pl.BlockSpec((WO_TM, d_model), lambda i: (i, 0))
    vec = pl.BlockSpec((1, d_model), lambda i: (0, 0))
    return pl.pallas_call(
        functools.partial(_wo_ln_body, alpha=alpha),
        grid=(n_rows // WO_TM,),
        in_specs=[row, pl.BlockSpec(w_o.shape, lambda i: (0, 0)), row,
                  pl.BlockSpec((1, 1, d_model), lambda i: (mod_row(i), 0, 0)), vec, vec],
        out_specs=row,
        out_shape=jax.ShapeDtypeStruct((n_rows, d_model), F32),
        compiler_params=_cparams("parallel"),
        name="wo_deepnorm",
    )(acc, w_o, x_all, g1, ln_g, ln_b)


def _topk_rows(s, n_out, rowi):
    n = s.shape[0]
    vals, poss = [], []
    for _ in range(n_out):
        m = jnp.max(s, axis=0, keepdims=True)
        pos = jnp.min(jnp.where(s == m, rowi, n), axis=0, keepdims=True)
        s = jnp.where(rowi == pos, -jnp.inf, s)
        vals.append(m)
        poss.append(pos)
    return vals, poss


CAND_COUNTS = tuple(PEER_TOPK // (a + 1) for a in range(PEER_TOPK))
N_CAND = sum(CAND_COUNTS)
N_CAND_PAD = -(-N_CAND // 8) * 8


def _peer_topk_body(q_ref, sk_ref, idx_ref, g_ref, ts_scr, ti_scr, cs_scr, ci_scr):
    k = PEER_TOPK
    row_keys = lax.broadcasted_iota(jnp.int32, (PEER_NKEYS, TOK), 0)
    row_cand = lax.broadcasted_iota(jnp.int32, (N_CAND_PAD, TOK), 0)
    cs_scr[N_CAND:, :] = jnp.full((N_CAND_PAD - N_CAND, TOK), -jnp.inf, F32)
    ci_scr[N_CAND:, :] = jnp.zeros((N_CAND_PAD - N_CAND, TOK), jnp.int32)

    def head(h, carry):
        for part in range(2):
            s_t = lax.dot_general(sk_ref[h, part], q_ref[2 * h + part],
                                  (((1,), (1,)), ((), ())), preferred_element_type=F32)
            vals, poss = _topk_rows(s_t, k, row_keys)
            for a in range(k):
                ts_scr[part, a:a + 1, :] = vals[a]
                ti_scr[part, a:a + 1, :] = poss[a]
        off = 0
        for a, nb in enumerate(CAND_COUNTS):
            cs_scr[off:off + nb, :] = ts_scr[0, a:a + 1, :] + ts_scr[1, 0:nb, :]
            ci_scr[off:off + nb, :] = ti_scr[0, a:a + 1, :] * PEER_NKEYS + ti_scr[1, 0:nb, :]
            off += nb
        vals, poss = _topk_rows(cs_scr[...], k, row_cand)
        cand_i = ci_scr[...]
        exps = [jnp.exp(v - vals[0]) for v in vals]
        denom = exps[0]
        for e in exps[1:]:
            denom = denom + e
        inv = 1.0 / denom
        for a in range(k):
            eid = jnp.sum(jnp.where(row_cand == poss[a], cand_i, 0), axis=0, keepdims=True)
            idx_ref[h, a:a + 1, :] = eid
            g_ref[h, a:a + 1, :] = exps[a] * inv
        return carry

    lax.fori_loop(0, PEER_HEADS, head, 0)


def _peer_topk(q, subkeys, *, n_rows):
    k = PEER_TOPK
    out_spec = pl.BlockSpec((PEER_HEADS, k, TOK), lambda t: (0, 0, t))
    idx, gsel = pl.pallas_call(
        _peer_topk_body,
        grid=(n_rows // TOK,),
        in_specs=[pl.BlockSpec((q.shape[0], TOK, LANES), lambda t: (0, t, 0)),
                  pl.BlockSpec(subkeys.shape, lambda t: (0, 0, 0, 0))],
        out_specs=[out_spec, out_spec],
        out_shape=[jax.ShapeDtypeStruct((PEER_HEADS, k, n_rows), jnp.int32),
                   jax.ShapeDtypeStruct((PEER_HEADS, k, n_rows), F32)],
        scratch_shapes=[pltpu.VMEM((2, k, TOK), F32), pltpu.VMEM((2, k, TOK), jnp.int32),
                        pltpu.VMEM((N_CAND_PAD, TOK), F32), pltpu.VMEM((N_CAND_PAD, TOK), jnp.int32)],
        compiler_params=_cparams("parallel"),
        name="peer_topk",
    )(q, subkeys)
    return idx.reshape(PEER_HEADS * k, n_rows), gsel.reshape(PEER_HEADS * k, n_rows)


HI16 = -65536


def _peer_pack_body(u_ref, v_ref, o_ref):
    ub = lax.bitcast_convert_type(u_ref[...].astype(BF16).astype(F32), jnp.int32)
    vb = lax.bitcast_convert_type(v_ref[...].astype(BF16).astype(F32), jnp.int32)
    o_ref[...] = (lax.shift_right_logical(ub, 16) | (vb & HI16)).reshape(o_ref.shape)


def _peer_pack(u, v, layer):
    _, n_exp, d_model = u.shape
    in_spec = pl.BlockSpec((None, PACK_TM, d_model), lambda i: (layer, i, 0))
    return pl.pallas_call(
        _peer_pack_body,
        grid=(n_exp // PACK_TM,),
        in_specs=[in_spec, in_spec],
        out_specs=pl.BlockSpec((PACK_TM, 1, d_model), lambda i: (i, 0, 0)),
        out_shape=jax.ShapeDtypeStruct((n_exp, 1, d_model), jnp.int32),
        compiler_params=_cparams("parallel"),
        name="peer_pack",
    )(u, v)


def _peer_gather_body(idx_ref, g_ref, x_ref, s2_ref, sh2_ref, g2_ref, lg_ref, lb_ref, uv_hbm, o_ref,
                      *scratch, alpha):
    bufs = scratch[:PEER_NBUF]
    h_scr, y_scr, sem = scratch[PEER_NBUF:]
    n_sel, d_model = bufs[0].shape
    sub = 8
    h_scr[...] = x_ref[...] * (1.0 + s2_ref[0]) + sh2_ref[0]

    def issue(t, slot):
        for k in range(n_sel):
            pltpu.make_async_copy(uv_hbm.at[idx_ref[t, k]], bufs[slot].at[pl.ds(k, 1)],
                                  sem.at[slot]).start(priority=k % 2)

    def wait(slot):
        pltpu.make_async_copy(uv_hbm.at[pl.ds(0, n_sel), 0], bufs[slot], sem.at[slot]).wait()

    lane = lax.broadcasted_iota(jnp.int32, (n_sel, PEER_TT), 1)

    def token(t, slot):
        wait(slot)
        issue(jnp.minimum(t + PEER_AHEAD, PEER_TT - 1), (slot + PEER_AHEAD) % PEER_NBUF)
        buf = bufs[slot]
        h = h_scr[pl.ds(t, 1), :]
        gcol = jnp.sum(jnp.where(lane == t, g_ref[...], 0.0), axis=1, keepdims=True)
        yacc = jnp.zeros((sub, d_model), F32)
        for j in range(n_sel // sub):
            w = buf[j * sub:(j + 1) * sub, :]
            u = lax.bitcast_convert_type(lax.shift_left(w, 16), F32)
            a = jnp.sum(u * h, axis=1, keepdims=True)
            act = 0.5 * a * (1.0 + lax.erf(a * 0.7071067811865476))
            v = lax.bitcast_convert_type(w & HI16, F32)
            yacc = yacc + v * (act * gcol[j * sub:(j + 1) * sub])
        y_scr[pl.ds(t, 1), :] = jnp.sum(yacc, axis=0, keepdims=True)

    def ring(i, carry):
        for s in range(PEER_NBUF):
            token(PEER_NBUF * i + s, s)
        return carry

    for s in range(PEER_AHEAD):
        issue(s, s)
    lax.fori_loop(0, PEER_TT // PEER_NBUF, ring, 0)
    for s in range(PEER_AHEAD):
        wait(s)
    z = alpha * x_ref[...] + g2_ref[0] * y_scr[...]
    o_ref[...] = _layer_norm_rows(z, lg_ref[...], lb_ref[...])


def _peer_gather(idx_t, g_t, x1, mods, ln_g, ln_b, uv, *, n_rows, mod_row, alpha):
    d_model = x1.shape[1]
    n_sel = idx_t.shape[1]
    row = pl.BlockSpec((PEER_TT, d_model), lambda t: (t, 0))
    vec = pl.BlockSpec((1, d_model), lambda t: (0, 0))
    mod = pl.BlockSpec((1, 1, d_model), lambda t: (mod_row(t), 0, 0))
    return pl.pallas_call(
        functools.partial(_peer_gather_body, alpha=alpha),
        grid=(n_rows // PEER_TT,),
        in_specs=[pl.BlockSpec((PEER_TT, n_sel), lambda t: (t, 0), memory_space=pltpu.SMEM),
                  pl.BlockSpec((n_sel, PEER_TT), lambda t: (0, t)),
                  row, mod, mod, mod, vec, vec, pl.BlockSpec(memory_space=pl.ANY)],
        out_specs=row,
        out_shape=jax.ShapeDtypeStruct((n_rows, d_model), F32),
        scratch_shapes=[pltpu.VMEM((n_sel, d_model), jnp.int32) for _ in range(PEER_NBUF)]
        + [pltpu.VMEM((PEER_TT, d_model), F32), pltpu.VMEM((PEER_TT, d_model), F32),
           pltpu.SemaphoreType.DMA((PEER_NBUF,))],
        compiler_params=_cparams("arbitrary"),
        name="peer_gather",
    )(idx_t, g_t, x1, *mods, ln_g, ln_b, uv)


def _rope_tables(seq):
    rows = seq // GRID_W
    row = jnp.repeat(jnp.arange(rows, dtype=F32), GRID_W)
    col = jnp.tile(jnp.arange(GRID_W, dtype=F32), rows)

    def tables(dim):
        half = dim // 2
        freq = ROPE_THETA ** (-jnp.arange(0, half, 2, dtype=F32) / half)
        ang = jnp.concatenate([row[:, None] * freq, col[:, None] * freq], axis=-1)
        return jnp.cos(ang), jnp.sin(ang)

    def with_identity(cos_t, sin_t):
        return (jnp.concatenate([cos_t, jnp.ones((TOK, LANES), F32)], axis=0),
                jnp.concatenate([sin_t, jnp.zeros((TOK, LANES), F32)], axis=0))

    cos_h, sin_h = tables(HEAD_DIM)
    cos_r, sin_r = tables(B_ROPE)
    cosh, sinh = with_identity(jnp.concatenate([cos_h, cos_h], -1), jnp.concatenate([-sin_h, sin_h], -1))
    cosr, sinr = with_identity(jnp.tile(cos_r, (1, 4)), jnp.tile(jnp.concatenate([-sin_r, sin_r], -1), (1, 2)))
    return cosh, sinh, cosr, sinr


def _permute_w_in(w):
    d_model = w.shape[0]
    a_w, a_kv_w = A_HEADS * HEAD_DIM, A_KV_HEADS * HEAD_DIM
    d_w, d_kv_w = D_HEADS * HEAD_DIM, D_KV_HEADS * HEAD_DIM
    off_ak = 0
    off_av = off_ak + a_kv_w
    off_bckv = off_av + a_kv_w
    off_bkr = off_bckv + B_KV_RANK
    off_dk = off_bkr + B_ROPE
    off_dv = off_dk + d_kv_w
    off_aq = off_dv + d_kv_w
    off_bcq = off_aq + a_w
    off_dq = off_bcq + B_Q_RANK
    off_glu = off_dq + d_w
    off_gate = off_glu + 2 * C_CH

    def seg(off, n):
        return w[:, off:off + n]

    out = jnp.concatenate(
        [seg(off_aq, a_w), seg(off_dq, d_w), seg(off_glu, 2 * C_CH), seg(off_gate, N_BRANCH * d_model),
         seg(off_ak, a_kv_w), seg(off_av, a_kv_w), seg(off_bckv, B_KV_RANK), seg(off_dk, d_kv_w), seg(off_dv, d_kv_w),
         seg(off_bkr, B_ROPE), jnp.zeros((d_model, KV_W - 1280 - B_ROPE), w.dtype), seg(off_bcq, B_Q_RANK)], axis=1)
    assert out.shape[1] == NP_COLS
    return out.astype(BF16)


def kernel(x, c, ctx, c_ctx, w_ada, b_ada, w_in, a_q_norm, a_k_norm, b_q_norm, b_w_q_up, b_kv_norm, b_w_kv_up,
           c_conv_w, c_conv_b, c_ln_g, c_ln_b, d_sink, w_br_a, w_br_b, w_br_c, w_br_d, w_o,
           ln1_g, ln1_b, ln2_g, ln2_b, peer_wq, peer_subkeys, peer_u, peer_v):
    nb, seq, d_model = x.shape
    depth = w_ada.shape[0]
    assert ctx.shape[1] == TOK and seq % MM_TM == 0 and (nb * TOK) % MM_TM == 0
    assert d_model == 2048 and P_GATE + N_BRANCH * d_model == P_KV
    rl, rc = nb * seq, nb * TOK
    r_all = rl + rc
    nql = seq // TOK
    alpha = (2 * depth) ** 0.25
    scale_hd = HEAD_DIM ** -0.5
    scale_mla = B_QK ** -0.5

    def mod_row(tile):
        return lambda i: jnp.where(i < rl // tile, i // (seq // tile), nb)

    tabs = _rope_tables(seq)
    x_all = jnp.concatenate([x.reshape(rl, d_model), ctx.reshape(rc, d_model)], axis=0)
    n_mod = 16
    cvec = jnp.concatenate([c, c_ctx[None, :], jnp.zeros((n_mod - nb - 1, d_model), F32)], axis=0)

    for l in range(depth):
        last = l == depth - 1
        n_rows = rl if last else r_all
        n_q_tiles = nql if last else nql + 1

        mod = _mm(cvec, w_ada, w_layer=l, n_rows=n_mod, tm=n_mod, tn=ADA_TN, out_dtype=F32, prologue="silu",
                  bias=b_ada[l][None, :], name="adaln")
        sh1, s1, g1, sh2, s2, g2 = [mod[:, i * d_model:(i + 1) * d_model].reshape(n_mod, 1, d_model)
                                    for i in range(6)]

        p = _mm(x_all, _permute_w_in(w_in[l]), n_rows=r_all, tm=MM_TM, tn=MM_TN, out_dtype=F32,
                prologue="mod", mod=(s1, sh1), mod_row=mod_row(MM_TM), name="in_proj")

        wq = b_w_q_up[l].reshape(B_Q_RANK, B_HEADS, B_QK)
        wq = jnp.concatenate([wq[:, :, :B_NOPE].reshape(B_Q_RANK, -1), wq[:, :, B_NOPE:].reshape(B_Q_RANK, -1)],
                             axis=1).astype(BF16)
        wkv = b_w_kv_up[l].reshape(B_KV_RANK, B_HEADS, B_NOPE + B_VDIM)
        wkv = jnp.concatenate([wkv[:, :, :B_NOPE].reshape(B_KV_RANK, -1), wkv[:, :, B_NOPE:].reshape(B_KV_RANK, -1)],
                              axis=1).astype(BF16)
        norms = (a_q_norm[l][None, :], a_k_norm[l][None, :], b_q_norm[l][None, :], b_kv_norm[l][None, :])
        qa, qb, qd, ka, va, kb, vb, kd, vd = _prep(p, tabs, norms, wq, wkv, nb=nb, seq=seq, n_lat_rows=rl,
                                                   n_rows=r_all)

        attn = functools.partial(_attention, nb=nb, seq=seq, n_q_tiles=n_q_tiles)
        br_a = attn(qa, ka, va, group=A_HEADS // A_KV_HEADS, dk=HEAD_DIM, scale=scale_hd, name="attn_axial")
        br_b = attn(qb, kb, vb, group=4, dk=2 * B_NOPE, scale=scale_mla, name="attn_mla")
        br_d = attn(qd, kd, vd, group=D_HEADS // D_KV_HEADS, dk=HEAD_DIM, scale=scale_hd, sink=d_sink[l],
                    name="attn_window")
        br_c = _conv(p, c_conv_w[l], c_conv_b[l][None, :], c_ln_g[l][None, :], c_ln_b[l][None, :],
                     seq=seq, n_lat_rows=rl, n_rows=n_rows)

        w_brs = [w.astype(BF16) for w in (w_br_a[l], w_br_b[l], w_br_c[l], w_br_d[l])]
        acc = _merge((br_a, br_b, br_c, br_d), w_brs, p, n_rows=n_rows, d_model=d_model)
        x1 = _wo_ln(acc, w_o[l].astype(BF16), x_all, g1, ln1_g[l][None, :], ln1_b[l][None, :],
                    n_rows=n_rows, mod_row=mod_row(WO_TM), alpha=alpha)

        q = _mm(x1, peer_wq[l].astype(BF16), n_rows=n_rows, tm=MM_TM, tn=MM_TN, out_dtype=BF16,
                prologue="mod", mod=(s2, sh2), mod_row=mod_row(MM_TM), split_cols=True, name="peer_query")
        idx, gsel = _peer_topk(q, peer_subkeys[l].astype(BF16), n_rows=n_rows)
        x_all = _peer_gather(idx.T, gsel, x1, (s2, sh2, g2), ln2_g[l][None, :], ln2_b[l][None, :],
                             _peer_pack(peer_u, peer_v, l), n_rows=n_rows,
                             mod_row=mod_row(PEER_TT), alpha=alpha)

    return x_all[:rl].reshape(nb, seq, d_model)
```

```python
import functools

import jax
import jax.numpy as jnp
from jax import lax
from jax.experimental import pallas as pl
from jax.experimental.pallas import tpu as pltpu

F32 = jnp.float32
BF16 = jnp.bfloat16

GRID_W = 64
HEAD_DIM = 128
ROPE_THETA = 10000.0
A_HEADS, A_KV_HEADS = 8, 2
B_HEADS, B_Q_RANK, B_KV_RANK, B_NOPE, B_ROPE, B_VDIM = 8, 512, 256, 128, 64, 128
B_QK = B_NOPE + B_ROPE
C_CH, CONV_K = 1024, 31
D_HEADS, D_KV_HEADS = 8, 2
WINDOW = 128
N_BRANCH = 4
PEER_HEADS, PEER_NKEYS, PEER_TOPK, PEER_QDIM = 8, 128, 16, 256
RMS_EPS = 1e-6
LN_EPS = 1e-5
NEG_INF = -1e30

LANES = 128
SUBLANES = 8
TOK = 256
MM_TM = 1024
MM_TN = 1024
MERGE_TM = 1024
MERGE_TN = 512
WO_TM = 512
PACK_TM = 256
ADA_TN = 512
PEER_TT = 256
PEER_NBUF = 8
PEER_AHEAD = 5
CONV_HALO = 16
VMEM_LIMIT = 56 * 1024 * 1024

P_AQ, P_DQ, P_GLU_A, P_GLU_G, P_GATE = 0, 1024, 2048, 3072, 4096
P_KV = 12288
KV_W = 1536
P_BCQ = 13824
NP_COLS = 14336


def _cparams(*sem):
    return pltpu.CompilerParams(dimension_semantics=sem, vmem_limit_bytes=VMEM_LIMIT)


def _layer_norm_rows(z, g, b):
    mu = jnp.mean(z, axis=-1, keepdims=True)
    zc = z - mu
    var = jnp.mean(zc * zc, axis=-1, keepdims=True)
    return zc * lax.rsqrt(var + LN_EPS) * g + b


def _mm_body(*refs, prologue, has_bias):
    refs = list(refs)
    x_ref = refs.pop(0)
    s_ref = sh_ref = b_ref = None
    if prologue == "mod":
        s_ref, sh_ref = refs.pop(0), refs.pop(0)
    w_ref = refs.pop(0)
    if has_bias:
        b_ref = refs.pop(0)
    o_ref, h_scr = refs

    @pl.when(pl.program_id(1) == 0)
    def _():
        xv = x_ref[...].astype(F32)
        if prologue == "mod":
            xv = xv * (1.0 + s_ref[0]) + sh_ref[0]
        elif prologue == "silu":
            xv = xv * jax.nn.sigmoid(xv)
        h_scr[...] = xv.astype(BF16)

    acc = jnp.dot(h_scr[...], w_ref[...].astype(BF16), preferred_element_type=F32)
    if has_bias:
        acc = acc + b_ref[...]
    if len(o_ref.shape) == 3:
        for c in range(o_ref.shape[0]):
            o_ref[c] = acc[:, c * LANES:(c + 1) * LANES].astype(o_ref.dtype)
    else:
        o_ref[...] = acc.astype(o_ref.dtype)


def _mm(x, w, *, n_rows, tm, tn, out_dtype, prologue="none", mod=None, mod_row=None, bias=None,
        split_cols=False, w_layer=None, name=None):
    k_dim = x.shape[1]
    n_dim = w.shape[-1]
    if split_cols:
        out_spec = pl.BlockSpec((tn // LANES, tm, LANES), lambda i, j: (j, i, 0))
        out_shape = jax.ShapeDtypeStruct((n_dim // LANES, n_rows, LANES), out_dtype)
    else:
        out_spec = pl.BlockSpec((tm, tn), lambda i, j: (i, j))
        out_shape = jax.ShapeDtypeStruct((n_rows, n_dim), out_dtype)
    in_specs = [pl.BlockSpec((tm, k_dim), lambda i, j: (i, 0))]
    args = [x]
    if prologue == "mod":
        spec = pl.BlockSpec((1, 1, k_dim), lambda i, j: (mod_row(i), 0, 0))
        in_specs += [spec, spec]
        args += list(mod)
    if w_layer is None:
        in_specs.append(pl.BlockSpec((k_dim, tn), lambda i, j: (0, j)))
    else:
        in_specs.append(pl.BlockSpec((None, k_dim, tn), lambda i, j: (w_layer, 0, j)))
    args.append(w)
    if bias is not None:
        in_specs.append(pl.BlockSpec((1, tn), lambda i, j: (0, j)))
        args.append(bias)
    return pl.pallas_call(
        functools.partial(_mm_body, prologue=prologue, has_bias=bias is not None),
        grid=(n_rows // tm, n_dim // tn),
        in_specs=in_specs,
        out_specs=out_spec,
        out_shape=out_shape,
        scratch_shapes=[pltpu.VMEM((tm, k_dim), BF16)],
        compiler_params=_cparams("parallel", "arbitrary"),
        name=name,
    )(*args)


def _prep_body(aq_ref, dq_ref, kv_ref, cq_ref, cosh_ref, sinh_ref, cosr_ref, sinr_ref,
               aqn_ref, akn_ref, bqn_ref, bkvn_ref, wq_ref, wkv_ref,
               qa_ref, qb_ref, qd_ref, ka_ref, va_ref, kb_ref, vb_ref, kd_ref, vd_ref):
    cosh, sinh = cosh_ref[...], sinh_ref[...]
    cosr, sinr = cosr_ref[...], sinr_ref[...]
    lane = lax.broadcasted_iota(jnp.int32, (TOK, LANES), 1)
    low32 = (lane % 64) < 32
    low64 = lane < 64

    def rope_head(xh):
        return xh * cosh + pltpu.roll(xh, 64, 1) * sinh

    def rope_r64(xc):
        sw = jnp.where(low32, pltpu.roll(xc, 96, 1), pltpu.roll(xc, 32, 1))
        return xc * cosr + sw * sinr

    def rms(xh, g):
        ms = jnp.mean(xh * xh, axis=-1, keepdims=True)
        return xh * lax.rsqrt(ms + RMS_EPS) * g

    aqn, akn = aqn_ref[...], akn_ref[...]
    for h in range(A_HEADS):
        sl = slice(h * HEAD_DIM, (h + 1) * HEAD_DIM)
        qa_ref[:, sl] = rope_head(rms(aq_ref[:, sl], aqn)).astype(BF16)
        qd_ref[:, sl] = rope_head(dq_ref[:, sl]).astype(BF16)
    for h in range(A_KV_HEADS):
        sl = slice(h * HEAD_DIM, (h + 1) * HEAD_DIM)
        ka_ref[h] = rope_head(rms(kv_ref[:, sl], akn)).astype(BF16)
        va_ref[h] = kv_ref[:, 256 + h * HEAD_DIM:256 + (h + 1) * HEAD_DIM].T.astype(BF16)
        kd_ref[h] = rope_head(kv_ref[:, 768 + h * HEAD_DIM:768 + (h + 1) * HEAD_DIM]).astype(BF16)
        vd_ref[h] = kv_ref[:, 1024 + h * HEAD_DIM:1024 + (h + 1) * HEAD_DIM].T.astype(BF16)

    ckv = rms(kv_ref[:, 512:768], bkvn_ref[...]).astype(BF16)
    kv_up = jnp.dot(ckv, wkv_ref[...], preferred_element_type=F32)
    kr = rope_r64(kv_ref[:, 1280:1408]).astype(BF16)
    for h in range(B_HEADS):
        kb_ref[h, :, 0:B_NOPE] = kv_up[:, h * B_NOPE:(h + 1) * B_NOPE].astype(BF16)
        kb_ref[h, :, B_NOPE:2 * B_NOPE] = kr
        vb_ref[h] = kv_up[:, 1024 + h * B_VDIM:1024 + (h + 1) * B_VDIM].T.astype(BF16)

    cq = rms(cq_ref[...], bqn_ref[...]).astype(BF16)
    q_up = jnp.dot(cq, wq_ref[...], preferred_element_type=F32)
    for c in range(B_HEADS // 2):
        rc = rope_r64(q_up[:, 1024 + c * LANES:1024 + (c + 1) * LANES])
        for half in range(2):
            h = 2 * c + half
            piece = rc if half == 0 else pltpu.roll(rc, 64, 1)
            qb_ref[:, h * 256:h * 256 + B_NOPE] = q_up[:, h * B_NOPE:(h + 1) * B_NOPE].astype(BF16)
            qb_ref[:, h * 256 + B_NOPE:(h + 1) * 256] = jnp.where(low64, piece, 0.0).astype(BF16)


def _prep(p, tabs, norms, wq, wkv, *, nb, seq, n_lat_rows, n_rows):
    nql = seq // TOK
    n_lat_tiles = n_lat_rows // TOK
    t_len = seq + TOK

    def colspec(width, off):
        return pl.BlockSpec((TOK, width), lambda t: (t, off // width))

    def tabspec():
        return pl.BlockSpec((TOK, LANES), lambda t: (jnp.where(t < n_lat_tiles, t % nql, nql), 0))

    def full(a):
        return pl.BlockSpec(a.shape, lambda t: (0,) * a.ndim)

    def kvspec(heads, d):
        return pl.BlockSpec(
            (None, heads, TOK, d),
            lambda t: (jnp.where(t < n_lat_tiles, t // nql, t - n_lat_tiles), 0,
                       jnp.where(t < n_lat_tiles, t % nql, nql), 0))

    def vtspec(heads):
        return pl.BlockSpec(
            (None, heads, None, HEAD_DIM, TOK),
            lambda t: (jnp.where(t < n_lat_tiles, t // nql, t - n_lat_tiles), 0,
                       jnp.where(t < n_lat_tiles, t % nql, nql), 0, 0))

    def rowspec(width):
        return pl.BlockSpec((TOK, width), lambda t: (t, 0))

    def kvshape(heads, d):
        return jax.ShapeDtypeStruct((nb, heads, t_len, d), BF16)

    def vtshape(heads):
        return jax.ShapeDtypeStruct((nb, heads, nql + 1, HEAD_DIM, TOK), BF16)

    return pl.pallas_call(
        _prep_body,
        grid=(n_rows // TOK,),
        in_specs=[colspec(1024, P_AQ), colspec(1024, P_DQ), colspec(KV_W, P_KV), colspec(512, P_BCQ),
                  tabspec(), tabspec(), tabspec(), tabspec(),
                  full(norms[0]), full(norms[1]), full(norms[2]), full(norms[3]), full(wq), full(wkv)],
        out_specs=[rowspec(1024), rowspec(2048), rowspec(1024),
                   kvspec(2, 128), vtspec(2), kvspec(8, 256), vtspec(8), kvspec(2, 128), vtspec(2)],
        out_shape=[jax.ShapeDtypeStruct((n_rows, 1024), BF16), jax.ShapeDtypeStruct((n_rows, 2048), BF16),
                   jax.ShapeDtypeStruct((n_rows, 1024), BF16),
                   kvshape(2, 128), vtshape(2), kvshape(8, 256), vtshape(8), kvshape(2, 128), vtshape(2)],
        compiler_params=_cparams("parallel"),
        name="qkv_prep",
    )(p, p, p, p, *tabs, *norms, wq, wkv)


LOG2E = 1.4426950408889634


def _softmax_steps(qs, ks, vts, c_exp, masks, carries):
    n = len(qs)
    ss = [lax.dot_general(ks[i], qs[i], (((1,), (1,)), ((), ())), preferred_element_type=F32) for i in range(n)]
    prs, heads = [], []
    for i in range(n):
        m, l, acc = carries[i]
        s = ss[i] if masks[i] is None else jnp.where(masks[i], ss[i], NEG_INF)
        m_new = jnp.maximum(m, jnp.max(s, axis=0, keepdims=True))
        alpha = jnp.exp2((m - m_new) * c_exp)
        pr = jnp.exp2((s - m_new) * c_exp)
        prs.append(pr.astype(BF16))
        heads.append((m_new, alpha * l + jnp.sum(pr, axis=0, keepdims=True), alpha * acc))
    out = []
    for i in range(n):
        acc = heads[i][2]
        for j, vt in enumerate(vts[i]):
            acc = acc + jnp.dot(vt, prs[i][j * TOK:(j + 1) * TOK], preferred_element_type=F32)
        out.append((heads[i][0], heads[i][1], acc))
    return out


def _attn_store(o_ref, g, carry):
    _, l, acc = carry
    o_ref[:, g * HEAD_DIM:(g + 1) * HEAD_DIM] = (acc * (1.0 / l)).T.astype(o_ref.dtype)


def _attn_dense_body(q_ref, k_ref, v_ref, o_ref, *, group, dk, scale, nql, shared_kv):
    qi = pl.program_id(2)
    c_exp = scale * LOG2E
    kc = max(n for n in (16, 8, 4, 2, 1) if nql % n == 0)

    def step(c, carry, n_blk=kc):
        off = pl.multiple_of(c * (n_blk * TOK), TOK)
        qs = [q_ref[:, g * dk:(g + 1) * dk] for g in range(group)]
        ks = [k_ref[pl.ds(off, n_blk * TOK), :] if shared_kv else k_ref[g, pl.ds(off, n_blk * TOK), :]
              for g in range(group)]
        vts = [[v_ref[c * n_blk + j] if shared_kv else v_ref[g, c * n_blk + j] for j in range(n_blk)]
               for g in range(group)]
        return tuple(_softmax_steps(qs, ks, vts, c_exp, [None] * group, list(carry)))

    init = tuple((jnp.full((1, TOK), NEG_INF, F32), jnp.zeros((1, TOK), F32), jnp.zeros((HEAD_DIM, TOK), F32))
                 for _ in range(group))
    res = lax.fori_loop(0, jnp.where(qi >= nql, 0, nql // kc), step, init)
    res = step(nql, res, n_blk=1)
    for g in range(group):
        _attn_store(o_ref, g, res[g])


def _attn_win_body(sink_ref, q_ref, k_ref, v_ref, o_ref, *, group, scale, nql):
    kvh = pl.program_id(1)
    qi = pl.program_id(2)
    is_lat = qi < nql
    c_exp = scale * LOG2E
    n_blk = 3
    kiota = lax.broadcasted_iota(jnp.int32, (n_blk * TOK, TOK), 0)
    qpos = qi * TOK + lax.broadcasted_iota(jnp.int32, (n_blk * TOK, TOK), 1)
    qs = [q_ref[:, g * HEAD_DIM:(g + 1) * HEAD_DIM] for g in range(group)]
    carries = [(jnp.full((1, TOK), sink_ref[kvh * group + g] * (1.0 / scale), F32), jnp.ones((1, TOK), F32),
                jnp.zeros((HEAD_DIM, TOK), F32)) for g in range(group)]
    c0 = jnp.clip(qi - 1, 0, nql - n_blk)
    k = k_ref[pl.ds(pl.multiple_of(c0 * TOK, TOK), n_blk * TOK), :]
    mask = jnp.logical_and(jnp.abs(qpos - (c0 * TOK + kiota)) <= WINDOW, is_lat)
    carries = _softmax_steps(qs, [k] * group, [[v_ref[c0 + j] for j in range(n_blk)]] * group, c_exp,
                             [mask] * group, carries)
    carries = _softmax_steps(qs, [k_ref[nql * TOK:(nql + 1) * TOK, :]] * group, [[v_ref[nql]]] * group, c_exp,
                             [None] * group, carries)
    for g in range(group):
        _attn_store(o_ref, g, carries[g])


def _attention(q, k, v, *, nb, seq, n_q_tiles, group, dk, scale, sink=None, name=None):
    nql = seq // TOK
    t_len = k.shape[2]
    n_heads = q.shape[1] // dk
    shared_kv = k.shape[1] * group == n_heads
    kvh = n_heads // group

    def qrow(b, qi):
        return jnp.where(qi < nql, b * nql + qi, nb * nql + b)

    q_spec = pl.BlockSpec((TOK, group * dk), lambda b, h, qi: (qrow(b, qi), h))
    kv_heads = None if shared_kv else group
    k_spec = pl.BlockSpec((None, kv_heads, t_len, dk), lambda b, h, qi: (b, h, 0, 0))
    v_spec = pl.BlockSpec((None, kv_heads, nql + 1, HEAD_DIM, TOK), lambda b, h, qi: (b, h, 0, 0, 0))
    o_spec = pl.BlockSpec((TOK, group * HEAD_DIM), lambda b, h, qi: (qrow(b, qi), h))
    n_rows_out = nb * seq + (nb * TOK if n_q_tiles > nql else 0)
    if sink is None:
        body = functools.partial(_attn_dense_body, group=group, dk=dk, scale=scale, nql=nql, shared_kv=shared_kv)
        in_specs, args = [q_spec, k_spec, v_spec], (q, k, v)
    else:
        body = functools.partial(_attn_win_body, group=group, scale=scale, nql=nql)
        in_specs = [pl.BlockSpec(memory_space=pltpu.SMEM), q_spec, k_spec, v_spec]
        args = (sink, q, k, v)
    return pl.pallas_call(
        body,
        grid=(nb, kvh, n_q_tiles),
        in_specs=in_specs,
        out_specs=o_spec,
        out_shape=jax.ShapeDtypeStruct((n_rows_out, kvh * group * HEAD_DIM), BF16),
        compiler_params=_cparams("parallel", "parallel", "arbitrary"),
        name=name,
    )(*args)


def _conv_body(a_ref, g_ref, ap_ref, gp_ref, an_ref, gn_ref, w_ref, cb_ref, lg_ref, lb_ref, o_ref, u_scr, sh_scr,
               y_scr, *, nql, n_lat_tiles):
    t = pl.program_id(0)
    is_lat = t < n_lat_tiles
    i = t % nql
    has_prev = jnp.logical_and(is_lat, i > 0)
    has_next = jnp.logical_and(is_lat, i < nql - 1)
    up = ap_ref[...] * jax.nn.sigmoid(gp_ref[...])
    un = an_ref[...] * jax.nn.sigmoid(gn_ref[...])
    u_scr[0:CONV_HALO, :] = jnp.where(has_prev, up, 0.0)
    u_scr[CONV_HALO:CONV_HALO + TOK, :] = a_ref[...] * jax.nn.sigmoid(g_ref[...])
    u_scr[CONV_HALO + TOK:, :] = jnp.where(has_next, un, 0.0)
    base = CONV_HALO - CONV_K // 2
    n_sh = sh_scr.shape[1]
    for r in range(1, SUBLANES):
        sh_scr[r - 1] = u_scr[r:r + n_sh, :]
    for lc in range(C_CH // LANES):
        ls = slice(lc * LANES, (lc + 1) * LANES)
        acc = jnp.zeros((TOK, LANES), F32)
        for k in range(CONV_K):
            q, r = divmod(base + k, SUBLANES)
            rows = slice(q * SUBLANES, q * SUBLANES + TOK)
            tap = u_scr[rows, ls] if r == 0 else sh_scr[r - 1, rows, ls]
            acc = acc + tap * w_ref[k:k + 1, ls]
        y_scr[:, ls] = acc + cb_ref[:, ls]
    y = _layer_norm_rows(y_scr[...], lg_ref[...], lb_ref[...])
    o_ref[...] = (y * jax.nn.sigmoid(y)).astype(o_ref.dtype)


def _conv(p, conv_w, conv_b, ln_g, ln_b, *, seq, n_lat_rows, n_rows):
    nql = seq // TOK
    hb = TOK // CONV_HALO
    last_hb = p.shape[0] // CONV_HALO - 1

    def cur(off):
        return pl.BlockSpec((TOK, C_CH), lambda t: (t, off // C_CH))

    def prev(off):
        return pl.BlockSpec((CONV_HALO, C_CH), lambda t: (jnp.maximum(t * hb - 1, 0), off // C_CH))

    def nxt(off):
        return pl.BlockSpec((CONV_HALO, C_CH), lambda t: (jnp.minimum((t + 1) * hb, last_hb), off // C_CH))

    def full(a):
        return pl.BlockSpec(a.shape, lambda t: (0,) * a.ndim)

    return pl.pallas_call(
        functools.partial(_conv_body, nql=nql, n_lat_tiles=n_lat_rows // TOK),
        grid=(n_rows // TOK,),
        in_specs=[cur(P_GLU_A), cur(P_GLU_G), prev(P_GLU_A), prev(P_GLU_G), nxt(P_GLU_A), nxt(P_GLU_G),
                  full(conv_w), full(conv_b), full(ln_g), full(ln_b)],
        out_specs=pl.BlockSpec((TOK, C_CH), lambda t: (t, 0)),
        out_shape=jax.ShapeDtypeStruct((n_rows, C_CH), BF16),
        scratch_shapes=[pltpu.VMEM((TOK + 2 * CONV_HALO, C_CH), F32),
                        pltpu.VMEM((SUBLANES - 1, TOK + 2 * CONV_HALO - SUBLANES, C_CH), F32),
                        pltpu.VMEM((TOK, C_CH), F32)],
        compiler_params=_cparams("parallel"),
        name="conformer_conv",
    )(p, p, p, p, p, p, conv_w, conv_b, ln_g, ln_b)


def _merge_body(xa_ref, xb_ref, xc_ref, xd_ref, wa_ref, wb_ref, wc_ref, wd_ref,
                ga_ref, gb_ref, gc_ref, gd_ref, o_ref):
    acc = None
    for x_ref, w_ref, g_ref in ((xa_ref, wa_ref, ga_ref), (xb_ref, wb_ref, gb_ref),
                                (xc_ref, wc_ref, gc_ref), (xd_ref, wd_ref, gd_ref)):
        term = jax.nn.sigmoid(g_ref[...]) * jnp.dot(x_ref[...], w_ref[...], preferred_element_type=F32)
        acc = term if acc is None else acc + term
    o_ref[...] = acc.astype(o_ref.dtype)


def _merge(branches, w_brs, p, *, n_rows, d_model):
    tm, tn = MERGE_TM, MERGE_TN
    x_spec = pl.BlockSpec((tm, 1024), lambda i, j: (i, 0))
    w_spec = pl.BlockSpec((1024, tn), lambda i, j: (0, j))

    def gate_spec(n):
        return pl.BlockSpec((tm, tn), lambda i, j: (i, (P_GATE + n * d_model) // tn + j))

    return pl.pallas_call(
        _merge_body,
        grid=(n_rows // tm, d_model // tn),
        in_specs=[x_spec] * 4 + [w_spec] * 4 + [gate_spec(n) for n in range(N_BRANCH)],
        out_specs=pl.BlockSpec((tm, tn), lambda i, j: (i, j)),
        out_shape=jax.ShapeDtypeStruct((n_rows, d_model), BF16),
        compiler_params=_cparams("parallel", "arbitrary"),
        name="branch_merge",
    )(*branches, *w_brs, p, p, p, p)


def _wo_ln_body(acc_ref, w_ref, x_ref, g1_ref, lg_ref, lb_ref, o_ref, *, alpha):
    mix = jnp.dot(acc_ref[...], w_ref[...], preferred_element_type=F32)
    z = alpha * x_ref[...] + g1_ref[0] * mix
    o_ref[...] = _layer_norm_rows(z, lg_ref[...], lb_ref[...])


def _wo_ln(acc, w_o, x_all, g1, ln_g, ln_b, *, n_rows, mod_row, alpha):
    d_model = w_o.shape[0]
    row = pl.BlockSpec((WO_TM, d_model), lambda i: (i, 0))
    vec = pl.BlockSpec((1, d_model), lambda i: (0, 0))
    return pl.pallas_call(
        functools.partial(_wo_ln_body, alpha=alpha),
        grid=(n_rows // WO_TM,),
        in_specs=[row, pl.BlockSpec(w_o.shape, lambda i: (0, 0), pipeline_mode=pl.Buffered(1)), row,
                  pl.BlockSpec((1, 1, d_model), lambda i: (mod_row(i), 0, 0)), vec, vec],
        out_specs=row,
        out_shape=jax.ShapeDtypeStruct((n_rows, d_model), F32),
        compiler_params=_cparams("parallel"),
        name="wo_deepnorm",
    )(acc, w_o, x_all, g1, ln_g, ln_b)


def _topk_rows(s, n_out, rowi):
    n = s.shape[0]
    vals, poss = [], []
    for _ in range(n_out):
        m = jnp.max(s, axis=0, keepdims=True)
        pos = jnp.min(jnp.where(s == m, rowi, n), axis=0, keepdims=True)
        s = jnp.where(rowi == pos, -jnp.inf, s)
        vals.append(m)
        poss.append(pos)
    return vals, poss


CAND_COUNTS = tuple(PEER_TOPK // (a + 1) for a in range(PEER_TOPK))
N_CAND = sum(CAND_COUNTS)
N_CAND_PAD = -(-N_CAND // 8) * 8


def _peer_topk_body(q_ref, sk_ref, idx_ref, g_ref, ts_scr, ti_scr, cs_scr, ci_scr):
    k = PEER_TOPK
    row_keys = lax.broadcasted_iota(jnp.int32, (PEER_NKEYS, TOK), 0)
    row_cand = lax.broadcasted_iota(jnp.int32, (N_CAND_PAD, TOK), 0)
    cs_scr[N_CAND:, :] = jnp.full((N_CAND_PAD - N_CAND, TOK), -jnp.inf, F32)
    ci_scr[N_CAND:, :] = jnp.zeros((N_CAND_PAD - N_CAND, TOK), jnp.int32)

    def head(h, carry):
        for part in range(2):
            s_t = lax.dot_general(sk_ref[h, part], q_ref[2 * h + part],
                                  (((1,), (1,)), ((), ())), preferred_element_type=F32)
            vals, poss = _topk_rows(s_t, k, row_keys)
            for a in range(k):
                ts_scr[part, a:a + 1, :] = vals[a]
                ti_scr[part, a:a + 1, :] = poss[a]
        off = 0
        for a, nb in enumerate(CAND_COUNTS):
            cs_scr[off:off + nb, :] = ts_scr[0, a:a + 1, :] + ts_scr[1, 0:nb, :]
            ci_scr[off:off + nb, :] = ti_scr[0, a:a + 1, :] * PEER_NKEYS + ti_scr[1, 0:nb, :]
            off += nb
        vals, poss = _topk_rows(cs_scr[...], k, row_cand)
        cand_i = ci_scr[...]
        exps = [jnp.exp(v - vals[0]) for v in vals]
        denom = exps[0]
        for e in exps[1:]:
            denom = denom + e
        inv = 1.0 / denom
        for a in range(k):
            eid = jnp.sum(jnp.where(row_cand == poss[a], cand_i, 0), axis=0, keepdims=True)
            idx_ref[h, a:a + 1, :] = eid
            g_ref[h, a:a + 1, :] = exps[a] * inv
        return carry

    lax.fori_loop(0, PEER_HEADS, head, 0)


def _peer_topk(q, subkeys, *, n_rows):
    k = PEER_TOPK
    out_spec = pl.BlockSpec((PEER_HEADS, k, TOK), lambda t: (0, 0, t))
    idx, gsel = pl.pallas_call(
        _peer_topk_body,
        grid=(n_rows // TOK,),
        in_specs=[pl.BlockSpec((q.shape[0], TOK, LANES), lambda t: (0, t, 0)),
                  pl.BlockSpec(subkeys.shape, lambda t: (0, 0, 0, 0))],
        out_specs=[out_spec, out_spec],
        out_shape=[jax.ShapeDtypeStruct((PEER_HEADS, k, n_rows), jnp.int32),
                   jax.ShapeDtypeStruct((PEER_HEADS, k, n_rows), F32)],
        scratch_shapes=[pltpu.VMEM((2, k, TOK), F32), pltpu.VMEM((2, k, TOK), jnp.int32),
                        pltpu.VMEM((N_CAND_PAD, TOK), F32), pltpu.VMEM((N_CAND_PAD, TOK), jnp.int32)],
        compiler_params=_cparams("parallel"),
        name="peer_topk",
    )(q, subkeys)
    return idx.reshape(PEER_HEADS * k, n_rows), gsel.reshape(PEER_HEADS * k, n_rows)


HI16 = -65536


def _peer_pack_body(u_ref, v_ref, o_ref):
    ub = lax.bitcast_convert_type(u_ref[...].astype(BF16).astype(F32), jnp.int32)
    vb = lax.bitcast_convert_type(v_ref[...].astype(BF16).astype(F32), jnp.int32)
    o_ref[...] = (lax.shift_right_logical(ub, 16) | (vb & HI16)).reshape(o_ref.shape)


def _peer_pack(u, v, layer):
    _, n_exp, d_model = u.shape
    in_spec = pl.BlockSpec((None, PACK_TM, d_model), lambda i: (layer, i, 0))
    return pl.pallas_call(
        _peer_pack_body,
        grid=(n_exp // PACK_TM,),
        in_specs=[in_spec, in_spec],
        out_specs=pl.BlockSpec((PACK_TM, 1, d_model), lambda i: (i, 0, 0)),
        out_shape=jax.ShapeDtypeStruct((n_exp, 1, d_model), jnp.int32),
        compiler_params=_cparams("parallel"),
        name="peer_pack",
    )(u, v)


def _peer_gather_body(idx_ref, g_ref, x_ref, s2_ref, sh2_ref, g2_ref, lg_ref, lb_ref, uv_hbm, o_ref,
                      *scratch, alpha):
    bufs = scratch[:PEER_NBUF]
    h_scr, y_scr, sem = scratch[PEER_NBUF:]
    n_sel, d_model = bufs[0].shape
    sub = 8
    h_scr[...] = x_ref[...] * (1.0 + s2_ref[0]) + sh2_ref[0]

    def issue(t, slot):
        for k in range(n_sel):
            pltpu.make_async_copy(uv_hbm.at[idx_ref[t, k]], bufs[slot].at[pl.ds(k, 1)],
                                  sem.at[slot]).start(priority=k % 2)

    def wait(slot):
        pltpu.make_async_copy(uv_hbm.at[pl.ds(0, n_sel), 0], bufs[slot], sem.at[slot]).wait()

    lane = lax.broadcasted_iota(jnp.int32, (n_sel, PEER_TT), 1)

    def token(t, slot):
        wait(slot)
        issue(jnp.minimum(t + PEER_AHEAD, PEER_TT - 1), (slot + PEER_AHEAD) % PEER_NBUF)
        buf = bufs[slot]
        h = h_scr[pl.ds(t, 1), :]
        gcol = jnp.sum(jnp.where(lane == t, g_ref[...], 0.0), axis=1, keepdims=True)
        yacc = jnp.zeros((sub, d_model), F32)
        for j in range(n_sel // sub):
            w = buf[j * sub:(j + 1) * sub, :]
            u = lax.bitcast_convert_type(lax.shift_left(w, 16), F32)
            a = jnp.sum(u * h, axis=1, keepdims=True)
            act = 0.5 * a * (1.0 + lax.erf(a * 0.7071067811865476))
            v = lax.bitcast_convert_type(w & HI16, F32)
            yacc = yacc + v * (act * gcol[j * sub:(j + 1) * sub])
        y_scr[pl.ds(t, 1), :] = jnp.sum(yacc, axis=0, keepdims=True)

    def ring(i, carry):
        for s in range(PEER_NBUF):
            token(PEER_NBUF * i + s, s)
        return carry

    for s in range(PEER_AHEAD):
        issue(s, s)
    lax.fori_loop(0, PEER_TT // PEER_NBUF, ring, 0)
    for s in range(PEER_AHEAD):
        wait(s)
    z = alpha * x_ref[...] + g2_ref[0] * y_scr[...]
    o_ref[...] = _layer_norm_rows(z, lg_ref[...], lb_ref[...])


def _peer_gather(idx_t, g_t, x1, mods, ln_g, ln_b, uv, *, n_rows, mod_row, alpha):
    d_model = x1.shape[1]
    n_sel = idx_t.shape[1]
    row = pl.BlockSpec((PEER_TT, d_model), lambda t: (t, 0))
    vec = pl.BlockSpec((1, d_model), lambda t: (0, 0))
    mod = pl.BlockSpec((1, 1, d_model), lambda t: (mod_row(t), 0, 0))
    return pl.pallas_call(
        functools.partial(_peer_gather_body, alpha=alpha),
        grid=(n_rows // PEER_TT,),
        in_specs=[pl.BlockSpec((PEER_TT, n_sel), lambda t: (t, 0), memory_space=pltpu.SMEM),
                  pl.BlockSpec((n_sel, PEER_TT), lambda t: (0, t)),
                  row, mod, mod, mod, vec, vec, pl.BlockSpec(memory_space=pl.ANY)],
        out_specs=row,
        out_shape=jax.ShapeDtypeStruct((n_rows, d_model), F32),
        scratch_shapes=[pltpu.VMEM((n_sel, d_model), jnp.int32) for _ in range(PEER_NBUF)]
        + [pltpu.VMEM((PEER_TT, d_model), F32), pltpu.VMEM((PEER_TT, d_model), F32),
           pltpu.SemaphoreType.DMA((PEER_NBUF,))],
        compiler_params=_cparams("arbitrary"),
        name="peer_gather",
    )(idx_t, g_t, x1, *mods, ln_g, ln_b, uv)


def _rope_tables(seq):
    rows = seq // GRID_W
    row = jnp.repeat(jnp.arange(rows, dtype=F32), GRID_W)
    col = jnp.tile(jnp.arange(GRID_W, dtype=F32), rows)

    def tables(dim):
        half = dim // 2
        freq = ROPE_THETA ** (-jnp.arange(0, half, 2, dtype=F32) / half)
        ang = jnp.concatenate([row[:, None] * freq, col[:, None] * freq], axis=-1)
        return jnp.cos(ang), jnp.sin(ang)

    def with_identity(cos_t, sin_t):
        return (jnp.concatenate([cos_t, jnp.ones((TOK, LANES), F32)], axis=0),
                jnp.concatenate([sin_t, jnp.zeros((TOK, LANES), F32)], axis=0))

    cos_h, sin_h = tables(HEAD_DIM)
    cos_r, sin_r = tables(B_ROPE)
    cosh, sinh = with_identity(jnp.concatenate([cos_h, cos_h], -1), jnp.concatenate([-sin_h, sin_h], -1))
    cosr, sinr = with_identity(jnp.tile(cos_r, (1, 4)), jnp.tile(jnp.concatenate([-sin_r, sin_r], -1), (1, 2)))
    return cosh, sinh, cosr, sinr


def _permute_w_in(w):
    d_model = w.shape[0]
    a_w, a_kv_w = A_HEADS * HEAD_DIM, A_KV_HEADS * HEAD_DIM
    d_w, d_kv_w = D_HEADS * HEAD_DIM, D_KV_HEADS * HEAD_DIM
    off_ak = 0
    off_av = off_ak + a_kv_w
    off_bckv = off_av + a_kv_w
    off_bkr = off_bckv + B_KV_RANK
    off_dk = off_bkr + B_ROPE
    off_dv = off_dk + d_kv_w
    off_aq = off_dv + d_kv_w
    off_bcq = off_aq + a_w
    off_dq = off_bcq + B_Q_RANK
    off_glu = off_dq + d_w
    off_gate = off_glu + 2 * C_CH

    def seg(off, n):
        return w[:, off:off + n]

    out = jnp.concatenate(
        [seg(off_aq, a_w), seg(off_dq, d_w), seg(off_glu, 2 * C_CH), seg(off_gate, N_BRANCH * d_model),
         seg(off_ak, a_kv_w), seg(off_av, a_kv_w), seg(off_bckv, B_KV_RANK), seg(off_dk, d_kv_w), seg(off_dv, d_kv_w),
         seg(off_bkr, B_ROPE), jnp.zeros((d_model, KV_W - 1280 - B_ROPE), w.dtype), seg(off_bcq, B_Q_RANK)], axis=1)
    assert out.shape[1] == NP_COLS
    return out.astype(BF16)


def kernel(x, c, ctx, c_ctx, w_ada, b_ada, w_in, a_q_norm, a_k_norm, b_q_norm, b_w_q_up, b_kv_norm, b_w_kv_up,
           c_conv_w, c_conv_b, c_ln_g, c_ln_b, d_sink, w_br_a, w_br_b, w_br_c, w_br_d, w_o,
           ln1_g, ln1_b, ln2_g, ln2_b, peer_wq, peer_subkeys, peer_u, peer_v):
    nb, seq, d_model = x.shape
    depth = w_ada.shape[0]
    assert ctx.shape[1] == TOK and seq % MM_TM == 0 and (nb * TOK) % MM_TM == 0
    assert d_model == 2048 and P_GATE + N_BRANCH * d_model == P_KV
    rl, rc = nb * seq, nb * TOK
    r_all = rl + rc
    nql = seq // TOK
    alpha = (2 * depth) ** 0.25
    scale_hd = HEAD_DIM ** -0.5
    scale_mla = B_QK ** -0.5

    def mod_row(tile):
        return lambda i: jnp.where(i < rl // tile, i // (seq // tile), nb)

    tabs = _rope_tables(seq)
    x_all = jnp.concatenate([x.reshape(rl, d_model), ctx.reshape(rc, d_model)], axis=0)
    n_mod = 16
    cvec = jnp.concatenate([c, c_ctx[None, :], jnp.zeros((n_mod - nb - 1, d_model), F32)], axis=0)

    for l in range(depth):
        last = l == depth - 1
        n_rows = rl if last else r_all
        n_q_tiles = nql if last else nql + 1

        mod = _mm(cvec, w_ada, w_layer=l, n_rows=n_mod, tm=n_mod, tn=ADA_TN, out_dtype=F32, prologue="silu",
                  bias=b_ada[l][None, :], name="adaln")
        sh1, s1, g1, sh2, s2, g2 = [mod[:, i * d_model:(i + 1) * d_model].reshape(n_mod, 1, d_model)
                                    for i in range(6)]

        p = _mm(x_all, _permute_w_in(w_in[l]), n_rows=r_all, tm=MM_TM, tn=MM_TN, out_dtype=F32,
                prologue="mod", mod=(s1, sh1), mod_row=mod_row(MM_TM), name="in_proj")

        wq = b_w_q_up[l].reshape(B_Q_RANK, B_HEADS, B_QK)
        wq = jnp.concatenate([wq[:, :, :B_NOPE].reshape(B_Q_RANK, -1), wq[:, :, B_NOPE:].reshape(B_Q_RANK, -1)],
                             axis=1).astype(BF16)
        wkv = b_w_kv_up[l].reshape(B_KV_RANK, B_HEADS, B_NOPE + B_VDIM)
        wkv = jnp.concatenate([wkv[:, :, :B_NOPE].reshape(B_KV_RANK, -1), wkv[:, :, B_NOPE:].reshape(B_KV_RANK, -1)],
                              axis=1).astype(BF16)
        norms = (a_q_norm[l][None, :], a_k_norm[l][None, :], b_q_norm[l][None, :], b_kv_norm[l][None, :])
        qa, qb, qd, ka, va, kb, vb, kd, vd = _prep(p, tabs, norms, wq, wkv, nb=nb, seq=seq, n_lat_rows=rl,
                                                   n_rows=r_all)

        attn = functools.partial(_attention, nb=nb, seq=seq, n_q_tiles=n_q_tiles)
        br_a = attn(qa, ka, va, group=A_HEADS // A_KV_HEADS, dk=HEAD_DIM, scale=scale_hd, name="attn_axial")
        br_b = attn(qb, kb, vb, group=4, dk=2 * B_NOPE, scale=scale_mla, name="attn_mla")
        br_d = attn(qd, kd, vd, group=D_HEADS // D_KV_HEADS, dk=HEAD_DIM, scale=scale_hd, sink=d_sink[l],
                    name="attn_window")
        br_c = _conv(p, c_conv_w[l], c_conv_b[l][None, :], c_ln_g[l][None, :], c_ln_b[l][None, :],
                     seq=seq, n_lat_rows=rl, n_rows=n_rows)

        w_brs = [w.astype(BF16) for w in (w_br_a[l], w_br_b[l], w_br_c[l], w_br_d[l])]
        acc = _merge((br_a, br_b, br_c, br_d), w_brs, p, n_rows=n_rows, d_model=d_model)
        x1 = _wo_ln(acc, w_o[l].astype(BF16), x_all, g1, ln1_g[l][None, :], ln1_b[l][None, :],
                    n_rows=n_rows, mod_row=mod_row(WO_TM), alpha=alpha)

        q = _mm(x1, peer_wq[l].astype(BF16), n_rows=n_rows, tm=MM_TM, tn=MM_TN, out_dtype=BF16,
                prologue="mod", mod=(s2, sh2), mod_row=mod_row(MM_TM), split_cols=True, name="peer_query")
        idx, gsel = _peer_topk(q, peer_subkeys[l].astype(BF16), n_rows=n_rows)
        x_all = _peer_gather(idx.T, gsel, x1, (s2, sh2, g2), ln2_g[l][None, :], ln2_b[l][None, :],
                             _peer_pack(peer_u, peer_v, l), n_rows=n_rows,
                             mod_row=mod_row(PEER_TT), alpha=alpha)

    return x_all[:rl].reshape(nb, seq, d_model)
```
